```python
import jax, jax.numpy as jnp
from jax import lax
import numpy as np

D_MODEL = 1024
BATCH = 32
SEQ = 2048
DEPTH = 1

CHUNK = 128
A_GROUPS = 8
A_GROUP_DIM = D_MODEL // A_GROUPS
A_WIDTH = A_GROUPS * A_GROUP_DIM
B_HEADS = 8
B_HEAD_DIM = D_MODEL // B_HEADS
B_WIDTH = B_HEADS * B_HEAD_DIM
Q_BLOCK = 128
EPS = 1e-6

IN_WIDTHS = (A_WIDTH, A_WIDTH, A_WIDTH, B_WIDTH, B_WIDTH, B_WIDTH, B_WIDTH, D_MODEL, D_MODEL)
IN_PROJ_WIDTH = sum(IN_WIDTHS)
SPLIT_POINTS = tuple(int(p) for p in np.cumsum(IN_WIDTHS)[:-1])

kernel_name = "hybrid_gmlp_stickbreaking_gated_block"


def rms_norm(x, gain):
    xf = x.astype(jnp.float32)
    y = xf * lax.rsqrt(jnp.mean(xf * xf, axis=-1, keepdims=True) + EPS)
    return (y * gain.astype(jnp.float32)).astype(x.dtype)


def chunked_spatial_gating(u, v, w_s, b_s):
    bsz, seq, _ = u.shape
    n_chunks = seq // CHUNK
    vr = v.reshape(bsz, n_chunks, CHUNK, A_GROUPS, A_GROUP_DIM)
    causal = jnp.tril(jnp.ones((CHUNK, CHUNK), dtype=bool))
    w = jnp.where(causal[None], w_s, 0.0).astype(v.dtype)
    mixed = jnp.einsum('gts,bnsgc->bntgc', w, vr)
    mixed = mixed + b_s.T.astype(v.dtype)[None, None, :, :, None]
    return u * mixed.reshape(bsz, seq, A_WIDTH)


def stick_breaking_attention(q, k, v):
    bsz, seq, n_heads, head_dim = q.shape
    n_blocks = seq // Q_BLOCK
    scale = head_dim ** -0.5
    q_blocks = q.reshape(bsz, n_blocks, Q_BLOCK, n_heads, head_dim).transpose(1, 0, 2, 3, 4)
    key_pos = jnp.arange(seq)

    def one_block(args):
        q_blk, blk_idx = args
        logits = jnp.einsum('bthd,bshd->bhts', q_blk, k).astype(jnp.float32) * scale
        q_pos = blk_idx * Q_BLOCK + jnp.arange(Q_BLOCK)
        causal = key_pos[None, :] < q_pos[:, None]
        log_beta = jax.nn.log_sigmoid(logits)
        log_one_minus = jnp.where(causal, log_beta - logits, 0.0)
        suffix = lax.cumsum(log_one_minus, axis=3, reverse=True) - log_one_minus
        weights = jnp.where(causal, jnp.exp(log_beta + suffix), 0.0)
        return jnp.einsum('bhts,bshd->bthd', weights.astype(v.dtype), v)

    out = lax.map(one_block, (q_blocks, jnp.arange(n_blocks)))
    return out.transpose(1, 0, 2, 3, 4).reshape(bsz, seq, n_heads, head_dim)


def setup_inputs(seed: int = 0) -> dict:
    key = jax.random.key(seed)
    ks = jax.random.split(key, 11)
    f32 = jnp.float32
    x = jax.random.normal(ks[0], (BATCH, SEQ, D_MODEL), f32)
    norm_in = 1.0 + 0.02 * jax.random.normal(ks[1], (DEPTH, D_MODEL), f32)
    w_in = jax.random.normal(ks[2], (DEPTH, D_MODEL, IN_PROJ_WIDTH), f32) * D_MODEL ** -0.5
    norm_v = 1.0 + 0.02 * jax.random.normal(ks[3], (DEPTH, A_WIDTH), f32)
    w_s = jax.random.normal(ks[4], (DEPTH, A_GROUPS, CHUNK, CHUNK), f32) * (0.5 * CHUNK ** -0.5)
    b_s = 1.0 + 0.02 * jax.random.normal(ks[5], (DEPTH, A_GROUPS, CHUNK), f32)
    w_o_gmlp = jax.random.normal(ks[6], (DEPTH, A_WIDTH, D_MODEL), f32) * A_WIDTH ** -0.5
    w_o_sb = jax.random.normal(ks[7], (DEPTH, B_WIDTH, D_MODEL), f32) * B_WIDTH ** -0.5
    w_out = jax.random.normal(ks[8], (DEPTH, D_MODEL, D_MODEL), f32) * D_MODEL ** -0.5
    norm_final = 1.0 + 0.02 * jax.random.normal(ks[9], (D_MODEL,), f32)
    return {"x": x, "norm_in": norm_in, "w_in": w_in, "norm_v": norm_v, "w_s": w_s,
            "b_s": b_s, "w_o_gmlp": w_o_gmlp, "w_o_sb": w_o_sb, "w_out": w_out,
            "norm_final": norm_final}


def reference(x, norm_in, w_in, norm_v, w_s, b_s, w_o_gmlp, w_o_sb, w_out, norm_final):
    bsz, seq, _ = x.shape
    for layer in range(DEPTH):
        h = rms_norm(x, norm_in[layer])
        proj = jnp.einsum('bsd,de->bse', h, w_in[layer])
        u_a, v_a, z_a, q_b, k_b, v_b, z_b, gate_a, gate_b = jnp.split(proj, SPLIT_POINTS, axis=-1)

        u_a = jax.nn.gelu(u_a)
        v_a = rms_norm(jax.nn.gelu(v_a), norm_v[layer])
        y_a = chunked_spatial_gating(u_a, v_a, w_s[layer], b_s[layer]) * jax.nn.silu(z_a)

        heads = lambda t: t.reshape(bsz, seq, B_HEADS, B_HEAD_DIM)
        y_b = stick_breaking_attention(heads(q_b), heads(k_b), heads(v_b)).reshape(bsz, seq, B_WIDTH)
        y_b = y_b * jax.nn.silu(z_b)

        p_a = jnp.einsum('bse,ed->bsd', y_a, w_o_gmlp[layer])
        p_b = jnp.einsum('bse,ed->bsd', y_b, w_o_sb[layer])
        merged = jax.nn.sigmoid(gate_a) * p_a + jax.nn.sigmoid(gate_b) * p_b
        x = x + jnp.einsum('bsd,de->bse', merged, w_out[layer])
    return rms_norm(x, norm_final)
```

```python
import functools

import jax
import jax.numpy as jnp
from jax import lax
from jax.experimental import pallas as pl
from jax.experimental.pallas import tpu as pltpu

D_MODEL = 1024
CHUNK = 128
A_GROUPS = 8
A_GROUP_DIM = D_MODEL // A_GROUPS
B_HEADS = 8
B_HEAD_DIM = D_MODEL // B_HEADS
EPS = 1e-6
N_SEGMENTS = 9
SEG_U, SEG_V, SEG_ZA, SEG_Q, SEG_K, SEG_VB, SEG_ZB, SEG_GA, SEG_GB = range(N_SEGMENTS)

TOKEN_TILE = 512
ATTN_TILE = 256
VMEM_LIMIT_BYTES = 56 * 1024 * 1024

F32 = jnp.float32
BF16 = jnp.bfloat16


def _rms_norm(x, gain):
    return x * lax.rsqrt(jnp.mean(x * x, axis=-1, keepdims=True) + EPS) * gain


def _in_proj_kernel(x_ref, norm_in_ref, w_in_ref, norm_v_ref, w_s_ref, b_ref, w_oa_ref,
                    gapa_ref, q_ref, k_ref, v_ref, zb_ref, gb_ref, ug_ref, ya_ref):
    tm = x_ref.shape[0]
    hb = _rms_norm(x_ref[...], norm_in_ref[...]).astype(BF16)

    def seg(i):
        w = w_in_ref[:, i * D_MODEL:(i + 1) * D_MODEL]
        return jnp.dot(hb, w, preferred_element_type=F32)

    vn = _rms_norm(jax.nn.gelu(seg(SEG_V)), norm_v_ref[...]).astype(BF16)
    ug_ref[...] = jax.nn.gelu(seg(SEG_U)) * jax.nn.silu(seg(SEG_ZA))

    row = lax.broadcasted_iota(jnp.int32, (CHUNK, CHUNK), 0)
    col = lax.broadcasted_iota(jnp.int32, (CHUNK, CHUNK), 1)
    causal = col <= row
    for g in range(A_GROUPS):
        w_g = jnp.where(causal, w_s_ref[g], 0.0).astype(BF16)
        cols = slice(g * A_GROUP_DIM, (g + 1) * A_GROUP_DIM)
        for c in range(tm // CHUNK):
            rows = slice(c * CHUNK, (c + 1) * CHUNK)
            mixed = jnp.dot(w_g, vn[rows, cols], preferred_element_type=F32) + b_ref[:, cols]
            ya_ref[rows, cols] = (ug_ref[rows, cols] * mixed).astype(BF16)

    p_a = jnp.dot(ya_ref[...], w_oa_ref[...], preferred_element_type=F32)
    gapa_ref[...] = (jax.nn.sigmoid(seg(SEG_GA)) * p_a).astype(BF16)

    q_ref[...] = (seg(SEG_Q) * (B_HEAD_DIM ** -0.5)).astype(BF16)
    k_ref[...] = seg(SEG_K).astype(BF16)
    v_ref[...] = seg(SEG_VB).astype(BF16)
    zb_ref[...] = jax.nn.silu(seg(SEG_ZB)).astype(BF16)
    gb_ref[...] = jax.nn.sigmoid(seg(SEG_GB)).astype(BF16)


def _attn_kernel(q_ref, k_ref, v_ref, zb_ref, o_ref):
    seq = q_ref.shape[0]
    t = ATTN_TILE
    row = lax.broadcasted_iota(jnp.int32, (t, t), 0)
    col = lax.broadcasted_iota(jnp.int32, (t, t), 1)
    strictly_before = col < row
    later = (row > col).astype(BF16)

    def tile(q, j, acc, carry, diagonal):
        k = k_ref[pl.ds(j * t, t), :]
        v = v_ref[pl.ds(j * t, t), :]
        z = lax.dot_general(q, k, (((1,), (1,)), ((), ())), preferred_element_type=F32)
        log_beta = jnp.minimum(z, 0.0) - jnp.log(1.0 + jnp.exp(-jnp.abs(z)))
        log_1m = log_beta - z
        if diagonal:
            log_1m = jnp.where(strictly_before, log_1m, 0.0)
        hi = log_1m.astype(BF16)
        lo = (log_1m - hi.astype(F32)).astype(BF16)
        suffix = (jnp.dot(hi, later, preferred_element_type=F32)
                  + jnp.dot(lo, later, preferred_element_type=F32))
        w = jnp.exp(log_beta + suffix + carry)
        if diagonal:
            w = jnp.where(strictly_before, w, 0.0)
        acc = acc + jnp.dot(w.astype(BF16), v, preferred_element_type=F32)
        carry = carry + jnp.sum(log_1m, axis=1, keepdims=True)
        return acc, carry

    for i in range(seq // t):
        rows = slice(i * t, (i + 1) * t)
        q = q_ref[rows, :]
        acc = jnp.zeros((t, B_HEAD_DIM), F32)
        carry = jnp.zeros((t, 1), F32)
        acc, carry = tile(q, i, acc, carry, diagonal=True)

        def body(jj, state, q=q, i=i):
            return tile(q, i - 1 - jj, state[0], state[1], diagonal=False)

        acc, carry = lax.fori_loop(0, i, body, (acc, carry))
        o_ref[rows, :] = (acc * zb_ref[rows, :].astype(F32)).astype(BF16)


def _out_kernel(x_ref, yb_ref, gapa_ref, gb_ref, w_ob_ref, w_out_ref, norm_ref, o_ref):
    p_b = jnp.dot(yb_ref[...], w_ob_ref[...], preferred_element_type=F32)
    merged = gapa_ref[...].astype(F32) + gb_ref[...].astype(F32) * p_b
    y = x_ref[...] + jnp.dot(merged.astype(BF16), w_out_ref[...], preferred_element_type=F32)
    o_ref[...] = _rms_norm(y, norm_ref[...])


def _resident(shape):
    return pl.BlockSpec(shape, lambda *_: (0,) * len(shape), pipeline_mode=pl.Buffered(1))


def kernel(x, norm_in, w_in, norm_v, w_s, b_s, w_o_gmlp, w_o_sb, w_out, norm_final):
    bsz, seq, d = x.shape
    assert d == D_MODEL and seq % ATTN_TILE == 0 and seq % CHUNK == 0
    tokens = bsz * seq
    tm = TOKEN_TILE
    assert tokens % tm == 0 and tm % CHUNK == 0
    x2 = x.reshape(tokens, d)
    b_full = jnp.repeat(b_s[0].T, A_GROUP_DIM, axis=1)

    tok_spec = pl.BlockSpec((tm, d), lambda i: (i, 0))
    act = jax.ShapeDtypeStruct((tokens, d), BF16)
    params = pltpu.CompilerParams(dimension_semantics=("arbitrary",),
                                  vmem_limit_bytes=VMEM_LIMIT_BYTES)

    gapa, q, k, v, zb, gb = pl.pallas_call(
        _in_proj_kernel,
        grid=(tokens // tm,),
        in_specs=[tok_spec,
                  _resident((1, d)),
                  _resident((d, N_SEGMENTS * d)),
                  _resident((1, d)),
                  _resident((A_GROUPS, CHUNK, CHUNK)),
                  _resident((CHUNK, d)),
                  _resident((d, d))],
        out_specs=[tok_spec] * 6,
        out_shape=[act] * 6,
        scratch_shapes=[pltpu.VMEM((tm, d), F32), pltpu.VMEM((tm, d), BF16)],
        compiler_params=params,
        name="in_proj",
    )(x2, norm_in[0][None], w_in[0].astype(BF16), norm_v[0][None], w_s[0], b_full,
      w_o_gmlp[0].astype(BF16))

    head_spec = pl.BlockSpec((seq, B_HEAD_DIM), lambda b, h: (b, h))
    yb = pl.pallas_call(
        _attn_kernel,
        grid=(bsz, B_HEADS),
        in_specs=[head_spec] * 4,
        out_specs=head_spec,
        out_shape=act,
        compiler_params=pltpu.CompilerParams(dimension_semantics=("arbitrary", "arbitrary"),
                                             vmem_limit_bytes=VMEM_LIMIT_BYTES),
        name="stick_breaking_attn",
    )(q, k, v, zb)

    out = pl.pallas_call(
        _out_kernel,
        grid=(tokens // tm,),
        in_specs=[tok_spec, tok_spec, tok_spec, tok_spec,
                  _resident((d, d)), _resident((d, d)), _resident((1, d))],
        out_specs=tok_spec,
        out_shape=jax.ShapeDtypeStruct((tokens, d), F32),
        compiler_params=params,
        name="out_proj",
    )(x2, yb, gapa, gb, w_o_sb[0].astype(BF16), w_out[0].astype(BF16), norm_final[None])
    return out.reshape(bsz, seq, d)
```

```python
import math

import jax
import jax.numpy as jnp
from jax import lax
from jax.experimental import pallas as pl
from jax.experimental.pallas import tpu as pltpu

D_MODEL = 1024
CHUNK = 128
A_GROUPS = 8
A_GROUP_DIM = D_MODEL // A_GROUPS
B_HEADS = 8
B_HEAD_DIM = D_MODEL // B_HEADS
EPS = 1e-6
N_SEGMENTS = 9
SEG_U, SEG_V, SEG_ZA, SEG_Q, SEG_K, SEG_VB, SEG_ZB, SEG_GA, SEG_GB = range(N_SEGMENTS)
LOG2_E = math.log2(math.e)

TOKEN_TILE = 512
ATTN_TILE = 256
ATTN_HEADS_PER_STEP = 4
VMEM_LIMIT_BYTES = 56 * 1024 * 1024

F32 = jnp.float32
BF16 = jnp.bfloat16


def _rms_norm(x, gain):
    return x * lax.rsqrt(jnp.mean(x * x, axis=-1, keepdims=True) + EPS) * gain


def _in_proj_kernel(x_ref, norm_in_ref, w_in_ref, norm_v_ref, w_s_ref, b_ref, w_oa_ref,
                    gapa_ref, q_ref, k_ref, v_ref, zb_ref, gb_ref, ug_ref, ya_ref):
    tm = x_ref.shape[0]
    hb = _rms_norm(x_ref[...], norm_in_ref[...]).astype(BF16)

    def seg(i):
        w = w_in_ref[:, i * D_MODEL:(i + 1) * D_MODEL]
        return jnp.dot(hb, w, preferred_element_type=F32)

    vn = _rms_norm(jax.nn.gelu(seg(SEG_V)), norm_v_ref[...]).astype(BF16)
    ug_ref[...] = jax.nn.gelu(seg(SEG_U)) * jax.nn.silu(seg(SEG_ZA))

    row = lax.broadcasted_iota(jnp.int32, (CHUNK, CHUNK), 0)
    col = lax.broadcasted_iota(jnp.int32, (CHUNK, CHUNK), 1)
    causal = col <= row
    for g in range(A_GROUPS):
        w_g = jnp.where(causal, w_s_ref[g], 0.0).astype(BF16)
        cols = slice(g * A_GROUP_DIM, (g + 1) * A_GROUP_DIM)
        for c in range(tm // CHUNK):
            rows = slice(c * CHUNK, (c + 1) * CHUNK)
            mixed = jnp.dot(w_g, vn[rows, cols], preferred_element_type=F32) + b_ref[:, cols]
            ya_ref[rows, cols] = (ug_ref[rows, cols] * mixed).astype(BF16)

    p_a = jnp.dot(ya_ref[...], w_oa_ref[...], preferred_element_type=F32)
    gapa_ref[...] = (jax.nn.sigmoid(seg(SEG_GA)) * p_a).astype(BF16)

    q_ref[...] = (seg(SEG_Q) * (B_HEAD_DIM ** -0.5 * LOG2_E)).astype(BF16)
    k_ref[...] = seg(SEG_K).astype(BF16)
    v_ref[...] = seg(SEG_VB).astype(BF16)
    zb_ref[...] = jax.nn.silu(seg(SEG_ZB)).astype(BF16)
    gb_ref[...] = jax.nn.sigmoid(seg(SEG_GB)).astype(BF16)


def _attn_kernel(q_ref, k_ref, v_ref, zb_ref, o_ref, acc_ref, carry_ref):
    seq, width = q_ref.shape
    heads = width // B_HEAD_DIM
    t = ATTN_TILE
    row = lax.broadcasted_iota(jnp.int32, (t, t), 0)
    col = lax.broadcasted_iota(jnp.int32, (t, t), 1)
    strictly_before = col < row
    neg_from = jnp.where(row >= col, -1.0, 0.0).astype(BF16)
    neg_from2 = jnp.concatenate([neg_from, neg_from], axis=0)

    def tiles(i, j, diagonal):
        hs = range(heads)
        cols = [slice(h * B_HEAD_DIM, (h + 1) * B_HEAD_DIM) for h in hs]
        z = [lax.dot_general(q_ref[i * t:(i + 1) * t, cols[h]], k_ref[pl.ds(j * t, t), cols[h]],
                             (((1,), (1,)), ((), ())), preferred_element_type=F32) for h in hs]
        split = []
        for h in hs:
            neg_abs = lax.bitcast_convert_type(
                lax.bitcast_convert_type(z[h], jnp.uint32) | jnp.uint32(0x80000000), F32)
            n = jnp.maximum(z[h], 0.0) + jnp.log(1.0 + jnp.exp2(neg_abs)) * LOG2_E
            if diagonal:
                n = jnp.where(strictly_before, n, 0.0)
            hi = lax.bitcast_convert_type(
                lax.bitcast_convert_type(n, jnp.uint32) & jnp.uint32(0xFFFF0000), F32)
            split.append(jnp.concatenate([hi.astype(BF16), (n - hi).astype(BF16)], axis=1))
        suffix = [jnp.dot(split[h], neg_from2, preferred_element_type=F32) for h in hs]
        w = []
        for h in hs:
            x = z[h] + suffix[h]
            if not diagonal:
                x = x + carry_ref[h]
            p = jnp.exp2(x)
            if diagonal:
                p = jnp.where(strictly_before, p, 0.0)
            w.append(p.astype(BF16))
        pv = [jnp.dot(w[h], v_ref[pl.ds(j * t, t), cols[h]], preferred_element_type=F32)
              for h in hs]
        for h in hs:
            tile_total = suffix[h][:, 0:1]
            if diagonal:
                acc_ref[h] = pv[h]
                carry_ref[h] = tile_total
            else:
                acc_ref[h] += pv[h]
                carry_ref[h] += tile_total

    for i in range(seq // t):
        tiles(i, i, diagonal=True)

        def body(jj, _, i=i):
            tiles(i, i - 1 - jj, diagonal=False)
            return 0

        lax.fori_loop(0, i, body, 0)
        rows = slice(i * t, (i + 1) * t)
        for h in range(heads):
            cols = slice(h * B_HEAD_DIM, (h + 1) * B_HEAD_DIM)
            o_ref[rows, cols] = (acc_ref[h] * zb_ref[rows, cols].astype(F32)).astype(BF16)


def _out_kernel(x_ref, yb_ref, gapa_ref, gb_ref, w_ob_ref, w_out_ref, norm_ref, o_ref):
    p_b = jnp.dot(yb_ref[...], w_ob_ref[...], preferred_element_type=F32)
    merged = gapa_ref[...].astype(F32) + gb_ref[...].astype(F32) * p_b
    y = x_ref[...] + jnp.dot(merged.astype(BF16), w_out_ref[...], preferred_element_type=F32)
    o_ref[...] = _rms_norm(y, norm_ref[...])


def _resident(shape):
    return pl.BlockSpec(shape, lambda *_: (0,) * len(shape), pipeline_mode=pl.Buffered(1))


def kernel(x, norm_in, w_in, norm_v, w_s, b_s, w_o_gmlp, w_o_sb, w_out, norm_final):
    bsz, seq, d = x.shape
    assert d == D_MODEL and seq % ATTN_TILE == 0 and seq % CHUNK == 0
    tokens = bsz * seq
    tm = TOKEN_TILE
    assert tokens % tm == 0 and tm % CHUNK == 0
    x2 = x.reshape(tokens, d)
    b_full = jnp.repeat(b_s[0].T, A_GROUP_DIM, axis=1)

    tok_spec = pl.BlockSpec((tm, d), lambda i: (i, 0))
    act = jax.ShapeDtypeStruct((tokens, d), BF16)
    params = pltpu.CompilerParams(dimension_semantics=("arbitrary",),
                                  vmem_limit_bytes=VMEM_LIMIT_BYTES)

    gapa, q, k, v, zb, gb = pl.pallas_call(
        _in_proj_kernel,
        grid=(tokens // tm,),
        in_specs=[tok_spec,
                  _resident((1, d)),
                  _resident((d, N_SEGMENTS * d)),
                  _resident((1, d)),
                  _resident((A_GROUPS, CHUNK, CHUNK)),
                  _resident((CHUNK, d)),
                  _resident((d, d))],
        out_specs=[tok_spec] * 6,
        out_shape=[act] * 6,
        scratch_shapes=[pltpu.VMEM((tm, d), F32), pltpu.VMEM((tm, d), BF16)],
        compiler_params=params,
        name="in_proj",
    )(x2, norm_in[0][None], w_in[0].astype(BF16), norm_v[0][None], w_s[0], b_full,
      w_o_gmlp[0].astype(BF16))

    hps = ATTN_HEADS_PER_STEP
    head_spec = pl.BlockSpec((seq, hps * B_HEAD_DIM), lambda b, h: (b, h))
    yb = pl.pallas_call(
        _attn_kernel,
        grid=(bsz, B_HEADS // hps),
        in_specs=[head_spec] * 4,
        out_specs=head_spec,
        out_shape=act,
        scratch_shapes=[pltpu.VMEM((hps, ATTN_TILE, B_HEAD_DIM), F32),
                        pltpu.VMEM((hps, ATTN_TILE, 1), F32)],
        compiler_params=pltpu.CompilerParams(dimension_semantics=("arbitrary", "arbitrary"),
                                             vmem_limit_bytes=VMEM_LIMIT_BYTES),
        name="stick_breaking_attn",
    )(q, k, v, zb)

    out = pl.pallas_call(
        _out_kernel,
        grid=(tokens // tm,),
        in_specs=[tok_spec, tok_spec, tok_spec, tok_spec,
                  _resident((d, d)), _resident((d, d)), _resident((1, d))],
        out_specs=tok_spec,
        out_shape=jax.ShapeDtypeStruct((tokens, d), F32),
        compiler_params=params,
        name="out_proj",
    )(x2, yb, gapa, gb, w_o_sb[0].astype(BF16), w_out[0].astype(BF16), norm_final[None])
    return out.reshape(bsz, seq, d)
```

```python
import math

import jax
import numpy as np
import jax.numpy as jnp
from jax import lax
from jax.experimental import pallas as pl
from jax.experimental.pallas import tpu as pltpu

D_MODEL = 1024
CHUNK = 128
A_GROUPS = 8
A_GROUP_DIM = D_MODEL // A_GROUPS
B_HEADS = 8
B_HEAD_DIM = D_MODEL // B_HEADS
EPS = 1e-6
N_SEGMENTS = 9
SEG_U, SEG_V, SEG_ZA, SEG_Q, SEG_K, SEG_VB, SEG_ZB, SEG_GA, SEG_GB = range(N_SEGMENTS)
LOG2_E = math.log2(math.e)

TOKEN_TILE = 512
ATTN_TILE = 256
ATTN_HEADS_PER_STEP = 4
ATTN_LOOP_UNROLL = 4
VMEM_LIMIT_BYTES = 56 * 1024 * 1024

F32 = jnp.float32
BF16 = jnp.bfloat16


def _rms_norm(x, gain):
    return x * lax.rsqrt(jnp.mean(x * x, axis=-1, keepdims=True) + EPS) * gain


def _in_proj_kernel(x_ref, norm_in_ref, w_in_ref, norm_v_ref, w_s_ref, b_ref, w_oa_ref,
                    gapa_ref, q_ref, k_ref, v_ref, zb_ref, gb_ref, ug_ref, ya_ref):
    tm = x_ref.shape[0]
    hb = _rms_norm(x_ref[...], norm_in_ref[...]).astype(BF16)

    def seg(i):
        w = w_in_ref[:, i * D_MODEL:(i + 1) * D_MODEL]
        return jnp.dot(hb, w, preferred_element_type=F32)

    vn = _rms_norm(jax.nn.gelu(seg(SEG_V)), norm_v_ref[...]).astype(BF16)
    ug_ref[...] = jax.nn.gelu(seg(SEG_U)) * jax.nn.silu(seg(SEG_ZA))

    row = lax.broadcasted_iota(jnp.int32, (CHUNK, CHUNK), 0)
    col = lax.broadcasted_iota(jnp.int32, (CHUNK, CHUNK), 1)
    causal = col <= row
    for g in range(A_GROUPS):
        w_g = jnp.where(causal, w_s_ref[g], 0.0).astype(BF16)
        cols = slice(g * A_GROUP_DIM, (g + 1) * A_GROUP_DIM)
        for c in range(tm // CHUNK):
            rows = slice(c * CHUNK, (c + 1) * CHUNK)
            mixed = jnp.dot(w_g, vn[rows, cols], preferred_element_type=F32) + b_ref[:, cols]
            ya_ref[rows, cols] = (ug_ref[rows, cols] * mixed).astype(BF16)

    p_a = jnp.dot(ya_ref[...], w_oa_ref[...], preferred_element_type=F32)
    gapa_ref[...] = (jax.nn.sigmoid(seg(SEG_GA)) * p_a).astype(BF16)

    q_ref[...] = (seg(SEG_Q) * (B_HEAD_DIM ** -0.5 * LOG2_E)).astype(BF16)
    k_ref[...] = seg(SEG_K).astype(BF16)
    v_ref[...] = seg(SEG_VB).astype(BF16)
    zb_ref[...] = jax.nn.silu(seg(SEG_ZB)).astype(BF16)
    gb_ref[...] = jax.nn.sigmoid(seg(SEG_GB)).astype(BF16)


def _tile_order(n_q):
    return [(i, i - d) for d in range(n_q) for i in range(d, n_q)]


def _attn_kernel(order_ref, q_ref, k_ref, v_ref, zb_ref, o_ref, acc_ref, carry_ref, z0_ref, z1_ref):
    seq, width = q_ref.shape
    heads = width // B_HEAD_DIM
    hs = range(heads)
    cols = [slice(h * B_HEAD_DIM, (h + 1) * B_HEAD_DIM) for h in hs]
    t = ATTN_TILE
    n_q = seq // t
    order = _tile_order(n_q)
    n_tiles = len(order)
    z_ref = (z0_ref, z1_ref)
    row = lax.broadcasted_iota(jnp.int32, (t, t), 0)
    col = lax.broadcasted_iota(jnp.int32, (t, t), 1)
    strictly_before = col < row
    neg_from = jnp.where(row >= col, -1.0, 0.0).astype(BF16)
    neg_from2 = jnp.concatenate([neg_from, neg_from], axis=0)

    def rows_of(idx):
        if isinstance(idx, int):
            return slice(idx * t, (idx + 1) * t)
        return pl.ds(pl.multiple_of(idx * t, t), t)

    def logits(slot, i, j):
        for h in hs:
            z_ref[slot][h] = lax.dot_general(
                q_ref[rows_of(i), cols[h]], k_ref[rows_of(j), cols[h]],
                (((1,), (1,)), ((), ())), preferred_element_type=F32)

    def split_neg_log_keep(slot, diagonal):
        out = []
        for h in hs:
            z = z_ref[slot][h]
            n = jnp.maximum(z, 0.0) + jnp.log(1.0 + jnp.exp2(-jnp.abs(z))) * LOG2_E
            if diagonal:
                n = jnp.where(strictly_before, n, 0.0)
            hi = n.astype(BF16)
            lo = (n - hi.astype(F32)).astype(BF16)
            out.append(jnp.concatenate([hi, lo], axis=1))
        return out

    def finish(slot, suffix, i, j, diagonal):
        w = []
        for h in hs:
            x = z_ref[slot][h] + suffix[h]
            if not diagonal:
                x = x + carry_ref[i * heads + h]
            p = jnp.exp2(x)
            if diagonal:
                p = jnp.where(strictly_before, p, 0.0)
            w.append(p.astype(BF16))
        pv = [jnp.dot(w[h], v_ref[rows_of(j), cols[h]], preferred_element_type=F32) for h in hs]
        for h in hs:
            tile_total = suffix[h][:, 0:1]
            if diagonal:
                acc_ref[i * heads + h] = pv[h]
                carry_ref[i * heads + h] = tile_total
            else:
                acc_ref[i * heads + h] += pv[h]
                carry_ref[i * heads + h] += tile_total

    def iteration(it, parity, tile_of, diagonal_of):
        cur, nxt = tile_of(it - 1), tile_of(it)
        if cur is not None:
            split = split_neg_log_keep(1 - parity, diagonal_of(it - 1))
            suffix = [jnp.dot(split[h], neg_from2, preferred_element_type=F32) for h in hs]
        if nxt is not None:
            logits(parity, *nxt)
        if cur is not None:
            finish(1 - parity, suffix, *cur, diagonal_of(it - 1))

    def static_tile(n):
        return order[n] if 0 <= n < n_tiles else None

    def static_diag(n):
        return n < n_q

    def traced_tile(n):
        return order_ref[0, n], order_ref[1, n]

    unroll = ATTN_LOOP_UNROLL
    loop_start = n_q + 1
    n_trips = max(0, (n_tiles - loop_start) // unroll)
    loop_end = loop_start + unroll * n_trips
    for it in range(0, loop_start):
        iteration(it, it % 2, static_tile, static_diag)

    def trip(p, _):
        for u in range(unroll):
            iteration(loop_start + unroll * p + u, (loop_start + u) % 2, traced_tile,
                      lambda n: False)
        return 0

    lax.fori_loop(0, n_trips, trip, 0)
    for it in range(loop_end, n_tiles + 1):
        iteration(it, it % 2, static_tile, static_diag)

    for i in range(n_q):
        for h in hs:
            o_ref[rows_of(i), cols[h]] = (acc_ref[i * heads + h]
                                          * zb_ref[rows_of(i), cols[h]].astype(F32)).astype(BF16)


def _out_kernel(x_ref, yb_ref, gapa_ref, gb_ref, w_ob_ref, w_out_ref, norm_ref, o_ref):
    p_b = jnp.dot(yb_ref[...], w_ob_ref[...], preferred_element_type=F32)
    merged = gapa_ref[...].astype(F32) + gb_ref[...].astype(F32) * p_b
    y = x_ref[...] + jnp.dot(merged.astype(BF16), w_out_ref[...], preferred_element_type=F32)
    o_ref[...] = _rms_norm(y, norm_ref[...])


def _resident(shape):
    return pl.BlockSpec(shape, lambda *_: (0,) * len(shape), pipeline_mode=pl.Buffered(1))


def kernel(x, norm_in, w_in, norm_v, w_s, b_s, w_o_gmlp, w_o_sb, w_out, norm_final):
    bsz, seq, d = x.shape
    assert d == D_MODEL and seq % ATTN_TILE == 0 and seq % CHUNK == 0
    tokens = bsz * seq
    tm = TOKEN_TILE
    assert tokens % tm == 0 and tm % CHUNK == 0
    x2 = x.reshape(tokens, d)
    b_full = jnp.repeat(b_s[0].T, A_GROUP_DIM, axis=1)

    tok_spec = pl.BlockSpec((tm, d), lambda i: (i, 0))
    act = jax.ShapeDtypeStruct((tokens, d), BF16)
    params = pltpu.CompilerParams(dimension_semantics=("arbitrary",),
                                  vmem_limit_bytes=VMEM_LIMIT_BYTES)

    gapa, q, k, v, zb, gb = pl.pallas_call(
        _in_proj_kernel,
        grid=(tokens // tm,),
        in_specs=[tok_spec,
                  _resident((1, d)),
                  _resident((d, N_SEGMENTS * d)),
                  _resident((1, d)),
                  _resident((A_GROUPS, CHUNK, CHUNK)),
                  _resident((CHUNK, d)),
                  _resident((d, d))],
        out_specs=[tok_spec] * 6,
        out_shape=[act] * 6,
        scratch_shapes=[pltpu.VMEM((tm, d), F32), pltpu.VMEM((tm, d), BF16)],
        compiler_params=params,
        name="in_proj",
    )(x2, norm_in[0][None], w_in[0].astype(BF16), norm_v[0][None], w_s[0], b_full,
      w_o_gmlp[0].astype(BF16))

    hps = ATTN_HEADS_PER_STEP
    n_q = seq // ATTN_TILE
    order = jnp.asarray(np.array(_tile_order(n_q), dtype=np.int32).T)
    head_spec = pl.BlockSpec((seq, hps * B_HEAD_DIM), lambda b, h, order_ref: (b, h))
    yb = pl.pallas_call(
        _attn_kernel,
        grid_spec=pltpu.PrefetchScalarGridSpec(
            num_scalar_prefetch=1,
            grid=(bsz, B_HEADS // hps),
            in_specs=[head_spec] * 4,
            out_specs=head_spec,
            scratch_shapes=[pltpu.VMEM((n_q * hps, ATTN_TILE, B_HEAD_DIM), F32),
                            pltpu.VMEM((n_q * hps, ATTN_TILE, 1), F32),
                            *[pltpu.VMEM((hps, ATTN_TILE, ATTN_TILE), F32)] * 2]),
        out_shape=act,
        compiler_params=pltpu.CompilerParams(dimension_semantics=("arbitrary", "arbitrary"),
                                             vmem_limit_bytes=VMEM_LIMIT_BYTES),
        name="stick_breaking_attn",
    )(order, q, k, v, zb)

    out = pl.pallas_call(
        _out_kernel,
        grid=(tokens // tm,),
        in_specs=[tok_spec, tok_spec, tok_spec, tok_spec,
                  _resident((d, d)), _resident((d, d)), _resident((1, d))],
        out_specs=tok_spec,
        out_shape=jax.ShapeDtypeStruct((tokens, d), F32),
        compiler_params=params,
        name="out_proj",
    )(x2, yb, gapa, gb, w_o_sb[0].astype(BF16), w_out[0].astype(BF16), norm_final[None])
    return out.reshape(bsz, seq, d)
```

```python
import math

import jax
import numpy as np
import jax.numpy as jnp
from jax import lax
from jax.experimental import pallas as pl
from jax.experimental.pallas import tpu as pltpu

D_MODEL = 1024
CHUNK = 128
A_GROUPS = 8
A_GROUP_DIM = D_MODEL // A_GROUPS
B_HEADS = 8
B_HEAD_DIM = D_MODEL // B_HEADS
EPS = 1e-6
N_SEGMENTS = 9
SEG_U, SEG_V, SEG_ZA, SEG_Q, SEG_K, SEG_VB, SEG_ZB, SEG_GA, SEG_GB = range(N_SEGMENTS)
LOG2_E = math.log2(math.e)

TOKEN_TILE = 512
OUT_SUBTILE = 256
ATTN_TILE = 256
ATTN_HEADS_PER_STEP = 4
ATTN_LOOP_UNROLL = 4
VMEM_LIMIT_BYTES = 56 * 1024 * 1024

F32 = jnp.float32
BF16 = jnp.bfloat16


def _rms_norm(x, gain):
    return x * lax.rsqrt(jnp.mean(x * x, axis=-1, keepdims=True) + EPS) * gain


def _in_proj_kernel(x_ref, norm_in_ref, w_in_ref, norm_v_ref, w_s_ref, b_ref, w_oa_ref,
                    gapa_ref, q_ref, k_ref, v_ref, zb_ref, gb_ref, ug_ref, ya_ref):
    tm = x_ref.shape[0]
    hb = _rms_norm(x_ref[...], norm_in_ref[...]).astype(BF16)

    def seg(i):
        w = w_in_ref[:, i * D_MODEL:(i + 1) * D_MODEL]
        return jnp.dot(hb, w, preferred_element_type=F32)

    v_a = seg(SEG_V)
    q_ref[...] = (seg(SEG_Q) * (B_HEAD_DIM ** -0.5 * LOG2_E)).astype(BF16)
    vn = _rms_norm(jax.nn.gelu(v_a), norm_v_ref[...]).astype(BF16)
    u_a = seg(SEG_U)
    k_ref[...] = seg(SEG_K).astype(BF16)
    z_a = seg(SEG_ZA)
    zb_ref[...] = jax.nn.silu(seg(SEG_ZB)).astype(BF16)
    ug_ref[...] = jax.nn.gelu(u_a) * jax.nn.silu(z_a)

    row = lax.broadcasted_iota(jnp.int32, (CHUNK, CHUNK), 0)
    col = lax.broadcasted_iota(jnp.int32, (CHUNK, CHUNK), 1)
    causal = col <= row
    for g in range(A_GROUPS):
        w_g = jnp.where(causal, w_s_ref[g], 0.0).astype(BF16)
        cols = slice(g * A_GROUP_DIM, (g + 1) * A_GROUP_DIM)
        for c in range(tm // CHUNK):
            rows = slice(c * CHUNK, (c + 1) * CHUNK)
            mixed = jnp.dot(w_g, vn[rows, cols], preferred_element_type=F32) + b_ref[:, cols]
            ya_ref[rows, cols] = (ug_ref[rows, cols] * mixed).astype(BF16)

    gb_ref[...] = jax.nn.sigmoid(seg(SEG_GB)).astype(BF16)
    p_a = jnp.dot(ya_ref[...], w_oa_ref[...], preferred_element_type=F32)
    gapa_ref[...] = (jax.nn.sigmoid(seg(SEG_GA)) * p_a).astype(BF16)
    v_ref[...] = seg(SEG_VB).astype(BF16)


def _tile_order(n_q):
    return [(i, i - d) for d in range(n_q) for i in range(d, n_q)]


def _attn_kernel(order_ref, q_ref, k_ref, v_ref, zb_ref, o_ref, acc_ref, carry_ref, z0_ref, z1_ref):
    seq, width = q_ref.shape
    heads = width // B_HEAD_DIM
    hs = range(heads)
    cols = [slice(h * B_HEAD_DIM, (h + 1) * B_HEAD_DIM) for h in hs]
    t = ATTN_TILE
    n_q = seq // t
    order = _tile_order(n_q)
    n_tiles = len(order)
    z_ref = (z0_ref, z1_ref)
    row = lax.broadcasted_iota(jnp.int32, (t, t), 0)
    col = lax.broadcasted_iota(jnp.int32, (t, t), 1)
    strictly_before = col < row
    neg_from = jnp.where(row >= col, -1.0, 0.0).astype(BF16)

    def rows_of(idx):
        if isinstance(idx, int):
            return slice(idx * t, (idx + 1) * t)
        return pl.ds(pl.multiple_of(idx * t, t), t)

    def logits(slot, i, j):
        for h in hs:
            z_ref[slot][h] = lax.dot_general(
                q_ref[rows_of(i), cols[h]], k_ref[rows_of(j), cols[h]],
                (((1,), (1,)), ((), ())), preferred_element_type=F32)

    def neg_log_keep(slot, diagonal):
        out = []
        for h in hs:
            z = z_ref[slot][h]
            n = jnp.maximum(z, 0.0) + jnp.log(1.0 + jnp.exp2(-jnp.abs(z))) * LOG2_E
            if diagonal:
                n = jnp.where(strictly_before, n, 0.0)
            out.append(n.astype(BF16))
        return out

    def finish(slot, suffix, i, j, diagonal):
        w = []
        for h in hs:
            x = z_ref[slot][h] + suffix[h]
            if not diagonal:
                x = x + carry_ref[i * heads + h]
            p = jnp.exp2(x)
            if diagonal:
                p = jnp.where(strictly_before, p, 0.0)
            w.append(p.astype(BF16))
        pv = [jnp.dot(w[h], v_ref[rows_of(j), cols[h]], preferred_element_type=F32) for h in hs]
        for h in hs:
            tile_total = suffix[h][:, 0:1]
            if diagonal:
                acc_ref[i * heads + h] = pv[h]
                carry_ref[i * heads + h] = tile_total
            else:
                acc_ref[i * heads + h] += pv[h]
                carry_ref[i * heads + h] += tile_total

    def iteration(it, parity, tile_of, diagonal_of):
        cur, nxt = tile_of(it - 1), tile_of(it)
        if nxt is not None:
            logits(parity, *nxt)
        if cur is not None:
            nl = neg_log_keep(1 - parity, diagonal_of(it - 1))
            suffix = [jnp.dot(nl[h], neg_from, preferred_element_type=F32) for h in hs]
            finish(1 - parity, suffix, *cur, diagonal_of(it - 1))

    def static_tile(n):
        return order[n] if 0 <= n < n_tiles else None

    def static_diag(n):
        return n < n_q

    def traced_tile(n):
        return order_ref[0, n], order_ref[1, n]

    unroll = ATTN_LOOP_UNROLL
    loop_start = n_q + 1
    n_trips = max(0, (n_tiles - loop_start) // unroll)
    loop_end = loop_start + unroll * n_trips
    for it in range(0, loop_start):
        iteration(it, it % 2, static_tile, static_diag)

    def trip(p, _):
        for u in range(unroll):
            iteration(loop_start + unroll * p + u, (loop_start + u) % 2, traced_tile,
                      lambda n: False)
        return 0

    lax.fori_loop(0, n_trips, trip, 0)
    for it in range(loop_end, n_tiles + 1):
        iteration(it, it % 2, static_tile, static_diag)

    for i in range(n_q):
        for h in hs:
            o_ref[rows_of(i), cols[h]] = (acc_ref[i * heads + h]
                                          * zb_ref[rows_of(i), cols[h]].astype(F32)).astype(BF16)


def _out_kernel(x_ref, yb_ref, gapa_ref, gb_ref, w_ob_ref, w_out_ref, norm_ref, o_ref):
    tm = x_ref.shape[0]
    rows = [slice(r, r + OUT_SUBTILE) for r in range(0, tm, OUT_SUBTILE)]
    p_b = [jnp.dot(yb_ref[r, :], w_ob_ref[...], preferred_element_type=F32) for r in rows]
    merged = [(gapa_ref[r, :].astype(F32) + gb_ref[r, :].astype(F32) * p).astype(BF16)
              for r, p in zip(rows, p_b)]
    update = [jnp.dot(m, w_out_ref[...], preferred_element_type=F32) for m in merged]
    for r, u in zip(rows, update):
        o_ref[r, :] = _rms_norm(x_ref[r, :] + u, norm_ref[...])


def _resident(shape):
    return pl.BlockSpec(shape, lambda *_: (0,) * len(shape), pipeline_mode=pl.Buffered(1))


def kernel(x, norm_in, w_in, norm_v, w_s, b_s, w_o_gmlp, w_o_sb, w_out, norm_final):
    bsz, seq, d = x.shape
    assert d == D_MODEL and seq % ATTN_TILE == 0 and seq % CHUNK == 0
    tokens = bsz * seq
    tm = TOKEN_TILE
    assert tokens % tm == 0 and tm % CHUNK == 0
    x2 = x.reshape(tokens, d)
    b_full = jnp.repeat(b_s[0].T, A_GROUP_DIM, axis=1)

    tok_spec = pl.BlockSpec((tm, d), lambda i: (i, 0))
    act = jax.ShapeDtypeStruct((tokens, d), BF16)
    params = pltpu.CompilerParams(dimension_semantics=("arbitrary",),
                                  vmem_limit_bytes=VMEM_LIMIT_BYTES)

    gapa, q, k, v, zb, gb = pl.pallas_call(
        _in_proj_kernel,
        grid=(tokens // tm,),
        in_specs=[tok_spec,
                  _resident((1, d)),
                  _resident((d, N_SEGMENTS * d)),
                  _resident((1, d)),
                  _resident((A_GROUPS, CHUNK, CHUNK)),
                  _resident((CHUNK, d)),
                  _resident((d, d))],
        out_specs=[tok_spec] * 6,
        out_shape=[act] * 6,
        scratch_shapes=[pltpu.VMEM((tm, d), F32), pltpu.VMEM((tm, d), BF16)],
        compiler_params=params,
        name="in_proj",
    )(x2, norm_in[0][None], w_in[0].astype(BF16), norm_v[0][None], w_s[0], b_full,
      w_o_gmlp[0].astype(BF16))

    hps = ATTN_HEADS_PER_STEP
    n_q = seq // ATTN_TILE
    order = jnp.asarray(np.array(_tile_order(n_q), dtype=np.int32).T)
    head_spec = pl.BlockSpec((seq, hps * B_HEAD_DIM), lambda b, h, order_ref: (b, h))
    yb = pl.pallas_call(
        _attn_kernel,
        grid_spec=pltpu.PrefetchScalarGridSpec(
            num_scalar_prefetch=1,
            grid=(bsz, B_HEADS // hps),
            in_specs=[head_spec] * 4,
            out_specs=head_spec,
            scratch_shapes=[pltpu.VMEM((n_q * hps, ATTN_TILE, B_HEAD_DIM), F32),
                            pltpu.VMEM((n_q * hps, ATTN_TILE, 1), F32),
                            *[pltpu.VMEM((hps, ATTN_TILE, ATTN_TILE), F32)] * 2]),
        out_shape=act,
        compiler_params=pltpu.CompilerParams(dimension_semantics=("arbitrary", "arbitrary"),
                                             vmem_limit_bytes=VMEM_LIMIT_BYTES),
        name="stick_breaking_attn",
    )(order, q, k, v, zb)

    out = pl.pallas_call(
        _out_kernel,
        grid=(tokens // tm,),
        in_specs=[tok_spec, tok_spec, tok_spec, tok_spec,
                  _resident((d, d)), _resident((d, d)), _resident((1, d))],
        out_specs=tok_spec,
        out_shape=jax.ShapeDtypeStruct((tokens, d), F32),
        compiler_params=params,
        name="out_proj",
    )(x2, yb, gapa, gb, w_o_sb[0].astype(BF16), w_out[0].astype(BF16), norm_final[None])
    return out.reshape(bsz, seq, d)
```

```python
import math

import jax
import numpy as np
import jax.numpy as jnp
from jax import lax
from jax.experimental import pallas as pl
from jax.experimental.pallas import tpu as pltpu

D_MODEL = 1024
CHUNK = 128
A_GROUPS = 8
A_GROUP_DIM = D_MODEL // A_GROUPS
B_HEADS = 8
B_HEAD_DIM = D_MODEL // B_HEADS
EPS = 1e-6
N_SEGMENTS = 9
SEG_U, SEG_V, SEG_ZA, SEG_Q, SEG_K, SEG_VB, SEG_ZB, SEG_GA, SEG_GB = range(N_SEGMENTS)
LOG2_E = math.log2(math.e)

TOKEN_TILE = 512
IN_SUBTILE = 256
OUT_TOKEN_TILE = 1024
OUT_SUBTILE = 256
ATTN_TILE = 256
ATTN_HEADS_PER_STEP = 4
ATTN_LOOP_UNROLL = 4
VMEM_LIMIT_BYTES = 56 * 1024 * 1024

F32 = jnp.float32
BF16 = jnp.bfloat16


def _rms_norm(x, gain):
    return x * lax.rsqrt(jnp.mean(x * x, axis=-1, keepdims=True) + EPS) * gain


def _in_proj_kernel(x_ref, norm_in_ref, w_in_ref, norm_v_ref, w_s_ref, b_ref, w_oa_ref,
                    gapa_ref, q_ref, k_ref, v_ref, zb_ref, gb_ref, ug_ref, ya_ref):
    tm = x_ref.shape[0]
    parts = [slice(r, r + IN_SUBTILE) for r in range(0, tm, IN_SUBTILE)]
    hb = [_rms_norm(x_ref[p, :], norm_in_ref[...]).astype(BF16) for p in parts]

    def seg(i):
        w = w_in_ref[:, i * D_MODEL:(i + 1) * D_MODEL]
        return [jnp.dot(h, w, preferred_element_type=F32) for h in hb]

    def store(ref, values, fn):
        for p, val in zip(parts, values):
            ref[p, :] = fn(val).astype(BF16)

    v_a = seg(SEG_V)
    store(q_ref, seg(SEG_Q), lambda s: s * (B_HEAD_DIM ** -0.5 * LOG2_E))
    vn = [_rms_norm(jax.nn.gelu(s), norm_v_ref[...]).astype(BF16) for s in v_a]
    u_a = seg(SEG_U)
    store(k_ref, seg(SEG_K), lambda s: s)
    z_a = seg(SEG_ZA)
    store(zb_ref, seg(SEG_ZB), jax.nn.silu)
    for p, u, z in zip(parts, u_a, z_a):
        ug_ref[p, :] = jax.nn.gelu(u) * jax.nn.silu(z)

    row = lax.broadcasted_iota(jnp.int32, (CHUNK, CHUNK), 0)
    col = lax.broadcasted_iota(jnp.int32, (CHUNK, CHUNK), 1)
    causal = col <= row
    for g in range(A_GROUPS):
        w_g = jnp.where(causal, w_s_ref[g], 0.0).astype(BF16)
        cols = slice(g * A_GROUP_DIM, (g + 1) * A_GROUP_DIM)
        for p, vn_p in zip(parts, vn):
            for c in range(0, IN_SUBTILE, CHUNK):
                rows = slice(p.start + c, p.start + c + CHUNK)
                mixed = (jnp.dot(w_g, vn_p[c:c + CHUNK, cols], preferred_element_type=F32)
                         + b_ref[:, cols])
                ya_ref[rows, cols] = (ug_ref[rows, cols] * mixed).astype(BF16)

    store(gb_ref, seg(SEG_GB), jax.nn.sigmoid)
    p_a = [jnp.dot(ya_ref[p, :], w_oa_ref[...], preferred_element_type=F32) for p in parts]
    store(gapa_ref, [jax.nn.sigmoid(g) * pa for g, pa in zip(seg(SEG_GA), p_a)], lambda s: s)
    store(v_ref, seg(SEG_VB), lambda s: s)


def _tile_order(n_q):
    return [(i, i - d) for d in range(n_q) for i in range(d, n_q)]


def _attn_kernel(order_ref, q_ref, k_ref, v_ref, zb_ref, o_ref, acc_ref, carry_ref, z0_ref, z1_ref):
    seq, width = q_ref.shape
    heads = width // B_HEAD_DIM
    hs = range(heads)
    cols = [slice(h * B_HEAD_DIM, (h + 1) * B_HEAD_DIM) for h in hs]
    t = ATTN_TILE
    n_q = seq // t
    order = _tile_order(n_q)
    n_tiles = len(order)
    z_ref = (z0_ref, z1_ref)
    row = lax.broadcasted_iota(jnp.int32, (t, t), 0)
    col = lax.broadcasted_iota(jnp.int32, (t, t), 1)
    strictly_before = col < row
    neg_from = jnp.where(row >= col, -1.0, 0.0).astype(BF16)

    def rows_of(idx):
        if isinstance(idx, int):
            return slice(idx * t, (idx + 1) * t)
        return pl.ds(pl.multiple_of(idx * t, t), t)

    def logits(slot, i, j):
        for h in hs:
            z_ref[slot][h] = lax.dot_general(
                q_ref[rows_of(i), cols[h]], k_ref[rows_of(j), cols[h]],
                (((1,), (1,)), ((), ())), preferred_element_type=F32)

    def neg_log_keep(slot, diagonal):
        out = []
        for h in hs:
            z = z_ref[slot][h]
            n = jnp.maximum(z, 0.0) + jnp.log(1.0 + jnp.exp2(-jnp.abs(z))) * LOG2_E
            if diagonal:
                n = jnp.where(strictly_before, n, 0.0)
            out.append(n.astype(BF16))
        return out

    def finish(slot, suffix, i, j, diagonal):
        w = []
        for h in hs:
            x = z_ref[slot][h] + suffix[h]
            if not diagonal:
                x = x + carry_ref[i * heads + h]
            p = jnp.exp2(x)
            if diagonal:
                p = jnp.where(strictly_before, p, 0.0)
            w.append(p.astype(BF16))
        pv = [jnp.dot(w[h], v_ref[rows_of(j), cols[h]], preferred_element_type=F32) for h in hs]
        for h in hs:
            tile_total = suffix[h][:, 0:1]
            if diagonal:
                acc_ref[i * heads + h] = pv[h]
                carry_ref[i * heads + h] = tile_total
            else:
                acc_ref[i * heads + h] += pv[h]
                carry_ref[i * heads + h] += tile_total

    def iteration(it, parity, tile_of, diagonal_of):
        cur, nxt = tile_of(it - 1), tile_of(it)
        if nxt is not None:
            logits(parity, *nxt)
        if cur is not None:
            nl = neg_log_keep(1 - parity, diagonal_of(it - 1))
            suffix = [jnp.dot(nl[h], neg_from, preferred_element_type=F32) for h in hs]
            finish(1 - parity, suffix, *cur, diagonal_of(it - 1))

    def static_tile(n):
        return order[n] if 0 <= n < n_tiles else None

    def static_diag(n):
        return n < n_q

    def traced_tile(n):
        return order_ref[0, n], order_ref[1, n]

    unroll = ATTN_LOOP_UNROLL
    loop_start = n_q + 1
    n_trips = max(0, (n_tiles - loop_start) // unroll)
    loop_end = loop_start + unroll * n_trips
    for it in range(0, loop_start):
        iteration(it, it % 2, static_tile, static_diag)

    def trip(p, _):
        for u in range(unroll):
            iteration(loop_start + unroll * p + u, (loop_start + u) % 2, traced_tile,
                      lambda n: False)
        return 0

    lax.fori_loop(0, n_trips, trip, 0)
    for it in range(loop_end, n_tiles + 1):
        iteration(it, it % 2, static_tile, static_diag)

    for i in range(n_q):
        for h in hs:
            o_ref[rows_of(i), cols[h]] = (acc_ref[i * heads + h]
                                          * zb_ref[rows_of(i), cols[h]].astype(F32)).astype(BF16)


def _out_kernel(x_ref, yb_ref, gapa_ref, gb_ref, w_ob_ref, w_out_ref, norm_ref, o_ref):
    tm = x_ref.shape[0]
    rows = [slice(r, r + OUT_SUBTILE) for r in range(0, tm, OUT_SUBTILE)]
    p_b = [jnp.dot(yb_ref[r, :], w_ob_ref[...], preferred_element_type=F32) for r in rows]
    merged = [(gapa_ref[r, :].astype(F32) + gb_ref[r, :].astype(F32) * p).astype(BF16)
              for r, p in zip(rows, p_b)]
    update = [jnp.dot(m, w_out_ref[...], preferred_element_type=F32) for m in merged]
    for r, u in zip(rows, update):
        o_ref[r, :] = _rms_norm(x_ref[r, :] + u, norm_ref[...])


def _resident(shape):
    return pl.BlockSpec(shape, lambda *_: (0,) * len(shape), pipeline_mode=pl.Buffered(1))


def kernel(x, norm_in, w_in, norm_v, w_s, b_s, w_o_gmlp, w_o_sb, w_out, norm_final):
    bsz, seq, d = x.shape
    assert d == D_MODEL and seq % ATTN_TILE == 0 and seq % CHUNK == 0
    tokens = bsz * seq
    tm = TOKEN_TILE
    assert tokens % tm == 0 and tm % CHUNK == 0
    x2 = x.reshape(tokens, d)
    b_full = jnp.repeat(b_s[0].T, A_GROUP_DIM, axis=1)

    tok_spec = pl.BlockSpec((tm, d), lambda i: (i, 0))
    act = jax.ShapeDtypeStruct((tokens, d), BF16)
    params = pltpu.CompilerParams(dimension_semantics=("arbitrary",),
                                  vmem_limit_bytes=VMEM_LIMIT_BYTES)

    gapa, q, k, v, zb, gb = pl.pallas_call(
        _in_proj_kernel,
        grid=(tokens // tm,),
        in_specs=[tok_spec,
                  _resident((1, d)),
                  _resident((d, N_SEGMENTS * d)),
                  _resident((1, d)),
                  _resident((A_GROUPS, CHUNK, CHUNK)),
                  _resident((CHUNK, d)),
                  _resident((d, d))],
        out_specs=[tok_spec] * 6,
        out_shape=[act] * 6,
        scratch_shapes=[pltpu.VMEM((tm, d), F32), pltpu.VMEM((tm, d), BF16)],
        compiler_params=params,
        name="in_proj",
    )(x2, norm_in[0][None], w_in[0].astype(BF16), norm_v[0][None], w_s[0], b_full,
      w_o_gmlp[0].astype(BF16))

    hps = ATTN_HEADS_PER_STEP
    n_q = seq // ATTN_TILE
    order = jnp.asarray(np.array(_tile_order(n_q), dtype=np.int32).T)
    head_spec = pl.BlockSpec((seq, hps * B_HEAD_DIM), lambda b, h, order_ref: (b, h))
    yb = pl.pallas_call(
        _attn_kernel,
        grid_spec=pltpu.PrefetchScalarGridSpec(
            num_scalar_prefetch=1,
            grid=(bsz, B_HEADS // hps),
            in_specs=[head_spec] * 4,
            out_specs=head_spec,
            scratch_shapes=[pltpu.VMEM((n_q * hps, ATTN_TILE, B_HEAD_DIM), F32),
                            pltpu.VMEM((n_q * hps, ATTN_TILE, 1), F32),
                            *[pltpu.VMEM((hps, ATTN_TILE, ATTN_TILE), F32)] * 2]),
        out_shape=act,
        compiler_params=pltpu.CompilerParams(dimension_semantics=("arbitrary", "arbitrary"),
                                             vmem_limit_bytes=VMEM_LIMIT_BYTES),
        name="stick_breaking_attn",
    )(order, q, k, v, zb)

    assert tokens % OUT_TOKEN_TILE == 0
    out_spec = pl.BlockSpec((OUT_TOKEN_TILE, d), lambda i: (i, 0))
    out = pl.pallas_call(
        _out_kernel,
        grid=(tokens // OUT_TOKEN_TILE,),
        in_specs=[out_spec, out_spec, out_spec, out_spec,
                  _resident((d, d)), _resident((d, d)), _resident((1, d))],
        out_specs=out_spec,
        out_shape=jax.ShapeDtypeStruct((tokens, d), F32),
        compiler_params=params,
        name="out_proj",
    )(x2, yb, gapa, gb, w_o_sb[0].astype(BF16), w_out[0].astype(BF16), norm_final[None])
    return out.reshape(bsz, seq, d)
```

```python
import math

import jax
import numpy as np
import jax.numpy as jnp
from jax import lax
from jax.experimental import pallas as pl
from jax.experimental.pallas import tpu as pltpu

D_MODEL = 1024
CHUNK = 128
A_GROUPS = 8
A_GROUP_DIM = D_MODEL // A_GROUPS
B_HEADS = 8
B_HEAD_DIM = D_MODEL // B_HEADS
EPS = 1e-6
N_SEGMENTS = 9
SEG_U, SEG_V, SEG_ZA, SEG_Q, SEG_K, SEG_VB, SEG_ZB, SEG_GA, SEG_GB = range(N_SEGMENTS)
LOG2_E = math.log2(math.e)
EXP2_CLAMP = 64.0

TOKEN_TILE = 512
IN_SUBTILE = 256
OUT_TOKEN_TILE = 1024
OUT_SUBTILE = 256
ATTN_TILE = 256
ATTN_HEADS_PER_STEP = 4
ATTN_LOOP_UNROLL = 4
VMEM_LIMIT_BYTES = 56 * 1024 * 1024

F32 = jnp.float32
BF16 = jnp.bfloat16


def _rms_norm(x, gain):
    return x * lax.rsqrt(jnp.mean(x * x, axis=-1, keepdims=True) + EPS) * gain


def _in_proj_kernel(x_ref, norm_in_ref, w_in_ref, norm_v_ref, w_s_ref, b_ref, w_oa_ref,
                    gapa_ref, q_ref, k_ref, v_ref, zb_ref, gb_ref, ug_ref, ya_ref):
    tm = x_ref.shape[0]
    parts = [slice(r, r + IN_SUBTILE) for r in range(0, tm, IN_SUBTILE)]
    hb = [_rms_norm(x_ref[p, :], norm_in_ref[...]).astype(BF16) for p in parts]

    def seg(i):
        w = w_in_ref[:, i * D_MODEL:(i + 1) * D_MODEL]
        return [jnp.dot(h, w, preferred_element_type=F32) for h in hb]

    def store(ref, values, fn):
        for p, val in zip(parts, values):
            ref[p, :] = fn(val).astype(BF16)

    v_a = seg(SEG_V)
    store(q_ref, seg(SEG_Q), lambda s: s * (B_HEAD_DIM ** -0.5 * LOG2_E))
    vn = [_rms_norm(jax.nn.gelu(s), norm_v_ref[...]).astype(BF16) for s in v_a]
    u_a = seg(SEG_U)
    store(k_ref, seg(SEG_K), lambda s: s)
    z_a = seg(SEG_ZA)
    store(zb_ref, seg(SEG_ZB), jax.nn.silu)
    for p, u, z in zip(parts, u_a, z_a):
        ug_ref[p, :] = jax.nn.gelu(u) * jax.nn.silu(z)

    row = lax.broadcasted_iota(jnp.int32, (CHUNK, CHUNK), 0)
    col = lax.broadcasted_iota(jnp.int32, (CHUNK, CHUNK), 1)
    causal = col <= row
    for g in range(A_GROUPS):
        w_g = jnp.where(causal, w_s_ref[g], 0.0).astype(BF16)
        cols = slice(g * A_GROUP_DIM, (g + 1) * A_GROUP_DIM)
        for p, vn_p in zip(parts, vn):
            for c in range(0, IN_SUBTILE, CHUNK):
                rows = slice(p.start + c, p.start + c + CHUNK)
                mixed = (jnp.dot(w_g, vn_p[c:c + CHUNK, cols], preferred_element_type=F32)
                         + b_ref[:, cols])
                ya_ref[rows, cols] = (ug_ref[rows, cols] * mixed).astype(BF16)

    store(gb_ref, seg(SEG_GB), jax.nn.sigmoid)
    p_a = [jnp.dot(ya_ref[p, :], w_oa_ref[...], preferred_element_type=F32) for p in parts]
    store(gapa_ref, [jax.nn.sigmoid(g) * pa for g, pa in zip(seg(SEG_GA), p_a)], lambda s: s)
    store(v_ref, seg(SEG_VB), lambda s: s)


def _tile_order(n_q):
    return [(i, i - d) for d in range(n_q) for i in range(d, n_q)]


def _attn_kernel(order_ref, q_ref, k_ref, v_ref, zb_ref, o_ref, acc_ref, carry_ref, z0_ref, z1_ref):
    seq, width = q_ref.shape
    heads = width // B_HEAD_DIM
    hs = range(heads)
    cols = [slice(h * B_HEAD_DIM, (h + 1) * B_HEAD_DIM) for h in hs]
    t = ATTN_TILE
    n_q = seq // t
    order = _tile_order(n_q)
    n_tiles = len(order)
    z_ref = (z0_ref, z1_ref)
    row = lax.broadcasted_iota(jnp.int32, (t, t), 0)
    col = lax.broadcasted_iota(jnp.int32, (t, t), 1)
    strictly_before = col < row
    neg_from = jnp.where(row >= col, -1.0, 0.0).astype(BF16)

    def rows_of(idx):
        if isinstance(idx, int):
            return slice(idx * t, (idx + 1) * t)
        return pl.ds(pl.multiple_of(idx * t, t), t)

    def logits(slot, i, j):
        for h in hs:
            z_ref[slot][h] = lax.dot_general(
                q_ref[rows_of(i), cols[h]], k_ref[rows_of(j), cols[h]],
                (((1,), (1,)), ((), ())), preferred_element_type=F32)

    def neg_log_keep(slot, diagonal):
        out = []
        for h in hs:
            z = z_ref[slot][h]
            n = jnp.maximum(z, jnp.log(1.0 + jnp.exp2(jnp.minimum(z, EXP2_CLAMP))) * LOG2_E)
            if diagonal:
                n = jnp.where(strictly_before, n, 0.0)
            out.append(n.astype(BF16))
        return out

    def finish(slot, suffix, i, j, diagonal):
        w = []
        for h in hs:
            x = z_ref[slot][h] + suffix[h]
            if not diagonal:
                x = x + carry_ref[i * heads + h]
            p = jnp.exp2(x)
            if diagonal:
                p = jnp.where(strictly_before, p, 0.0)
            w.append(p.astype(BF16))
        pv = [jnp.dot(w[h], v_ref[rows_of(j), cols[h]], preferred_element_type=F32) for h in hs]
        for h in hs:
            tile_total = suffix[h][:, 0:1]
            if diagonal:
                acc_ref[i * heads + h] = pv[h]
                carry_ref[i * heads + h] = tile_total
            else:
                acc_ref[i * heads + h] += pv[h]
                carry_ref[i * heads + h] += tile_total

    def iteration(it, parity, tile_of, diagonal_of):
        cur, nxt = tile_of(it - 1), tile_of(it)
        if nxt is not None:
            logits(parity, *nxt)
        if cur is not None:
            nl = neg_log_keep(1 - parity, diagonal_of(it - 1))
            suffix = [jnp.dot(nl[h], neg_from, preferred_element_type=F32) for h in hs]
            finish(1 - parity, suffix, *cur, diagonal_of(it - 1))

    def static_tile(n):
        return order[n] if 0 <= n < n_tiles else None

    def static_diag(n):
        return n < n_q

    def traced_tile(n):
        return order_ref[0, n], order_ref[1, n]

    unroll = ATTN_LOOP_UNROLL
    loop_start = n_q + 1
    n_trips = max(0, (n_tiles - loop_start) // unroll)
    loop_end = loop_start + unroll * n_trips
    for it in range(0, loop_start):
        iteration(it, it % 2, static_tile, static_diag)

    def trip(p, _):
        for u in range(unroll):
            iteration(loop_start + unroll * p + u, (loop_start + u) % 2, traced_tile,
                      lambda n: False)
        return 0

    lax.fori_loop(0, n_trips, trip, 0)
    for it in range(loop_end, n_tiles + 1):
        iteration(it, it % 2, static_tile, static_diag)

    for i in range(n_q):
        for h in hs:
            o_ref[rows_of(i), cols[h]] = (acc_ref[i * heads + h]
                                          * zb_ref[rows_of(i), cols[h]].astype(F32)).astype(BF16)


def _out_kernel(x_ref, yb_ref, gapa_ref, gb_ref, w_ob_ref, w_out_ref, norm_ref, o_ref):
    tm = x_ref.shape[0]
    rows = [slice(r, r + OUT_SUBTILE) for r in range(0, tm, OUT_SUBTILE)]
    p_b = [jnp.dot(yb_ref[r, :], w_ob_ref[...], preferred_element_type=F32) for r in rows]
    merged = [(gapa_ref[r, :].astype(F32) + gb_ref[r, :].astype(F32) * p).astype(BF16)
              for r, p in zip(rows, p_b)]
    update = [jnp.dot(m, w_out_ref[...], preferred_element_type=F32) for m in merged]
    for r, u in zip(rows, update):
        o_ref[r, :] = _rms_norm(x_ref[r, :] + u, norm_ref[...])


def _resident(shape):
    return pl.BlockSpec(shape, lambda *_: (0,) * len(shape), pipeline_mode=pl.Buffered(1))


def kernel(x, norm_in, w_in, norm_v, w_s, b_s, w_o_gmlp, w_o_sb, w_out, norm_final):
    bsz, seq, d = x.shape
    assert d == D_MODEL and seq % ATTN_TILE == 0 and seq % CHUNK == 0
    tokens = bsz * seq
    tm = TOKEN_TILE
    assert tokens % tm == 0 and tm % CHUNK == 0
    x2 = x.reshape(tokens, d)
    b_full = jnp.repeat(b_s[0].T, A_GROUP_DIM, axis=1)

    tok_spec = pl.BlockSpec((tm, d), lambda i: (i, 0))
    act = jax.ShapeDtypeStruct((tokens, d), BF16)
    params = pltpu.CompilerParams(dimension_semantics=("arbitrary",),
                                  vmem_limit_bytes=VMEM_LIMIT_BYTES)

    gapa, q, k, v, zb, gb = pl.pallas_call(
        _in_proj_kernel,
        grid=(tokens // tm,),
        in_specs=[tok_spec,
                  _resident((1, d)),
                  _resident((d, N_SEGMENTS * d)),
                  _resident((1, d)),
                  _resident((A_GROUPS, CHUNK, CHUNK)),
                  _resident((CHUNK, d)),
                  _resident((d, d))],
        out_specs=[tok_spec] * 6,
        out_shape=[act] * 6,
        scratch_shapes=[pltpu.VMEM((tm, d), F32), pltpu.VMEM((tm, d), BF16)],
        compiler_params=params,
        name="in_proj",
    )(x2, norm_in[0][None], w_in[0].astype(BF16), norm_v[0][None], w_s[0], b_full,
      w_o_gmlp[0].astype(BF16))

    hps = ATTN_HEADS_PER_STEP
    n_q = seq // ATTN_TILE
    order = jnp.asarray(np.array(_tile_order(n_q), dtype=np.int32).T)
    head_spec = pl.BlockSpec((seq, hps * B_HEAD_DIM), lambda b, h, order_ref: (b, h))
    yb = pl.pallas_call(
        _attn_kernel,
        grid_spec=pltpu.PrefetchScalarGridSpec(
            num_scalar_prefetch=1,
            grid=(bsz, B_HEADS // hps),
            in_specs=[head_spec] * 4,
            out_specs=head_spec,
            scratch_shapes=[pltpu.VMEM((n_q * hps, ATTN_TILE, B_HEAD_DIM), F32),
                            pltpu.VMEM((n_q * hps, ATTN_TILE, 1), F32),
                            *[pltpu.VMEM((hps, ATTN_TILE, ATTN_TILE), F32)] * 2]),
        out_shape=act,
        compiler_params=pltpu.CompilerParams(dimension_semantics=("arbitrary", "arbitrary"),
                                             vmem_limit_bytes=VMEM_LIMIT_BYTES),
        name="stick_breaking_attn",
    )(order, q, k, v, zb)

    assert tokens % OUT_TOKEN_TILE == 0
    out_spec = pl.BlockSpec((OUT_TOKEN_TILE, d), lambda i: (i, 0))
    out = pl.pallas_call(
        _out_kernel,
        grid=(tokens // OUT_TOKEN_TILE,),
        in_specs=[out_spec, out_spec, out_spec, out_spec,
                  _resident((d, d)), _resident((d, d)), _resident((1, d))],
        out_specs=out_spec,
        out_shape=jax.ShapeDtypeStruct((tokens, d), F32),
        compiler_params=params,
        name="out_proj",
    )(x2, yb, gapa, gb, w_o_sb[0].astype(BF16), w_out[0].astype(BF16), norm_final[None])
    return out.reshape(bsz, seq, d)
```

```python
import math

import jax
import numpy as np
import jax.numpy as jnp
from jax import lax
from jax.experimental import pallas as pl
from jax.experimental.pallas import tpu as pltpu

D_MODEL = 1024
CHUNK = 128
A_GROUPS = 8
A_GROUP_DIM = D_MODEL // A_GROUPS
B_HEADS = 8
B_HEAD_DIM = D_MODEL // B_HEADS
EPS = 1e-6
N_SEGMENTS = 9
SEG_U, SEG_V, SEG_ZA, SEG_Q, SEG_K, SEG_VB, SEG_ZB, SEG_GA, SEG_GB = range(N_SEGMENTS)
LOG2_E = math.log2(math.e)
EXP2_CLAMP = 64.0

TOKEN_TILE = 512
IN_SUBTILE = 256
OUT_TOKEN_TILE = 1024
OUT_SUBTILE = 256
ATTN_TILE = 256
ATTN_HEADS_PER_STEP = 4
ATTN_LOOP_UNROLL = 4
VMEM_LIMIT_BYTES = 56 * 1024 * 1024

F32 = jnp.float32
BF16 = jnp.bfloat16


def _rms_norm(x, gain):
    return x * lax.rsqrt(jnp.mean(x * x, axis=-1, keepdims=True) + EPS) * gain


def _in_proj_kernel(x_ref, norm_in_ref, w_in_ref, norm_v_ref, w_s_ref, b_ref, w_oa_ref,
                    gapa_ref, q_ref, k_ref, v_ref, zb_ref, gb_ref, ug_ref, ya_ref):
    tm = x_ref.shape[0]
    parts = [slice(r, r + IN_SUBTILE) for r in range(0, tm, IN_SUBTILE)]
    hb = [_rms_norm(x_ref[p, :], norm_in_ref[...]).astype(BF16) for p in parts]

    def seg(i):
        w = w_in_ref[:, i * D_MODEL:(i + 1) * D_MODEL]
        return [jnp.dot(h, w, preferred_element_type=F32) for h in hb]

    def store(ref, values, fn):
        for p, val in zip(parts, values):
            ref[p, :] = fn(val).astype(BF16)

    v_a = seg(SEG_V)
    u_a = seg(SEG_U)
    z_a = seg(SEG_ZA)
    store(q_ref, seg(SEG_Q), lambda s: s * (B_HEAD_DIM ** -0.5 * LOG2_E))
    store(k_ref, seg(SEG_K), lambda s: s)
    vn = [_rms_norm(jax.nn.gelu(s), norm_v_ref[...]).astype(BF16) for s in v_a]
    for p, u, z in zip(parts, u_a, z_a):
        ug_ref[p, :] = jax.nn.gelu(u) * jax.nn.silu(z)

    row = lax.broadcasted_iota(jnp.int32, (CHUNK, CHUNK), 0)
    col = lax.broadcasted_iota(jnp.int32, (CHUNK, CHUNK), 1)
    causal = col <= row
    for g in range(A_GROUPS):
        w_g = jnp.where(causal, w_s_ref[g], 0.0).astype(BF16)
        cols = slice(g * A_GROUP_DIM, (g + 1) * A_GROUP_DIM)
        for p, vn_p in zip(parts, vn):
            for c in range(0, IN_SUBTILE, CHUNK):
                rows = slice(p.start + c, p.start + c + CHUNK)
                mixed = (jnp.dot(w_g, vn_p[c:c + CHUNK, cols], preferred_element_type=F32)
                         + b_ref[:, cols])
                ya_ref[rows, cols] = (ug_ref[rows, cols] * mixed).astype(BF16)

    store(zb_ref, seg(SEG_ZB), jax.nn.silu)
    store(gb_ref, seg(SEG_GB), jax.nn.sigmoid)
    p_a = [jnp.dot(ya_ref[p, :], w_oa_ref[...], preferred_element_type=F32) for p in parts]
    store(gapa_ref, [jax.nn.sigmoid(g) * pa for g, pa in zip(seg(SEG_GA), p_a)], lambda s: s)
    store(v_ref, seg(SEG_VB), lambda s: s)


def _tile_order(n_q):
    return [(i, i - d) for d in range(n_q) for i in range(d, n_q)]


def _attn_kernel(order_ref, q_ref, k_ref, v_ref, zb_ref, o_ref, acc_ref, carry_ref, z0_ref, z1_ref):
    seq, width = q_ref.shape
    heads = width // B_HEAD_DIM
    hs = range(heads)
    cols = [slice(h * B_HEAD_DIM, (h + 1) * B_HEAD_DIM) for h in hs]
    t = ATTN_TILE
    n_q = seq // t
    order = _tile_order(n_q)
    n_tiles = len(order)
    z_ref = (z0_ref, z1_ref)
    row = lax.broadcasted_iota(jnp.int32, (t, t), 0)
    col = lax.broadcasted_iota(jnp.int32, (t, t), 1)
    strictly_before = col < row
    neg_from = jnp.where(row >= col, -1.0, 0.0).astype(BF16)

    def rows_of(idx):
        if isinstance(idx, int):
            return slice(idx * t, (idx + 1) * t)
        return pl.ds(pl.multiple_of(idx * t, t), t)

    def logits(slot, h, i, j):
        z_ref[slot][h] = lax.dot_general(
            q_ref[rows_of(i), cols[h]], k_ref[rows_of(j), cols[h]],
            (((1,), (1,)), ((), ())), preferred_element_type=F32)

    def neg_log_keep(slot, diagonal):
        out = []
        for h in hs:
            z = z_ref[slot][h]
            n = jnp.maximum(z, jnp.log(1.0 + jnp.exp2(jnp.minimum(z, EXP2_CLAMP))) * LOG2_E)
            if diagonal:
                n = jnp.where(strictly_before, n, 0.0)
            out.append(n.astype(BF16))
        return out

    def finish(slot, suffix, i, j, diagonal):
        w = []
        for h in hs:
            x = z_ref[slot][h] + suffix[h]
            if not diagonal:
                x = x + carry_ref[i * heads + h]
            p = jnp.exp2(x)
            if diagonal:
                p = jnp.where(strictly_before, p, 0.0)
            w.append(p.astype(BF16))
        pv = [jnp.dot(w[h], v_ref[rows_of(j), cols[h]], preferred_element_type=F32) for h in hs]
        for h in hs:
            tile_total = suffix[h][:, 0:1]
            if diagonal:
                acc_ref[i * heads + h] = pv[h]
                carry_ref[i * heads + h] = tile_total
            else:
                acc_ref[i * heads + h] += pv[h]
                carry_ref[i * heads + h] += tile_total

    def iteration(it, parity, tile_of, diagonal_of):
        cur, nxt = tile_of(it - 1), tile_of(it)
        if cur is not None:
            nl = neg_log_keep(1 - parity, diagonal_of(it - 1))
        suffix = []
        for h in hs:
            if cur is not None:
                suffix.append(jnp.dot(nl[h], neg_from, preferred_element_type=F32))
            if nxt is not None:
                logits(parity, h, *nxt)
        if cur is not None:
            finish(1 - parity, suffix, *cur, diagonal_of(it - 1))

    def static_tile(n):
        return order[n] if 0 <= n < n_tiles else None

    def static_diag(n):
        return n < n_q

    def traced_tile(n):
        return order_ref[0, n], order_ref[1, n]

    unroll = ATTN_LOOP_UNROLL
    loop_start = n_q + 1
    n_trips = max(0, (n_tiles - loop_start) // unroll)
    loop_end = loop_start + unroll * n_trips
    for it in range(0, loop_start):
        iteration(it, it % 2, static_tile, static_diag)

    def trip(p, _):
        for u in range(unroll):
            iteration(loop_start + unroll * p + u, (loop_start + u) % 2, traced_tile,
                      lambda n: False)
        return 0

    lax.fori_loop(0, n_trips, trip, 0)
    for it in range(loop_end, n_tiles + 1):
        iteration(it, it % 2, static_tile, static_diag)

    for i in range(n_q):
        for h in hs:
            o_ref[rows_of(i), cols[h]] = (acc_ref[i * heads + h]
                                          * zb_ref[rows_of(i), cols[h]].astype(F32)).astype(BF16)


def _out_kernel(x_ref, yb_ref, gapa_ref, gb_ref, w_ob_ref, w_out_ref, norm_ref, o_ref):
    tm = x_ref.shape[0]
    rows = [slice(r, r + OUT_SUBTILE) for r in range(0, tm, OUT_SUBTILE)]
    p_b = [jnp.dot(yb_ref[r, :], w_ob_ref[...], preferred_element_type=F32) for r in rows]
    merged = [(gapa_ref[r, :].astype(F32) + gb_ref[r, :].astype(F32) * p).astype(BF16)
              for r, p in zip(rows, p_b)]
    update = [jnp.dot(m, w_out_ref[...], preferred_element_type=F32) for m in merged]
    for r, u in zip(rows, update):
        o_ref[r, :] = _rms_norm(x_ref[r, :] + u, norm_ref[...])


def _resident(shape):
    return pl.BlockSpec(shape, lambda *_: (0,) * len(shape), pipeline_mode=pl.Buffered(1))


def kernel(x, norm_in, w_in, norm_v, w_s, b_s, w_o_gmlp, w_o_sb, w_out, norm_final):
    bsz, seq, d = x.shape
    assert d == D_MODEL and seq % ATTN_TILE == 0 and seq % CHUNK == 0
    tokens = bsz * seq
    tm = TOKEN_TILE
    assert tokens % tm == 0 and tm % CHUNK == 0
    x2 = x.reshape(tokens, d)
    b_full = jnp.repeat(b_s[0].T, A_GROUP_DIM, axis=1)

    tok_spec = pl.BlockSpec((tm, d), lambda i: (i, 0))
    act = jax.ShapeDtypeStruct((tokens, d), BF16)
    params = pltpu.CompilerParams(dimension_semantics=("arbitrary",),
                                  vmem_limit_bytes=VMEM_LIMIT_BYTES)

    gapa, q, k, v, zb, gb = pl.pallas_call(
        _in_proj_kernel,
        grid=(tokens // tm,),
        in_specs=[tok_spec,
                  _resident((1, d)),
                  _resident((d, N_SEGMENTS * d)),
                  _resident((1, d)),
                  _resident((A_GROUPS, CHUNK, CHUNK)),
                  _resident((CHUNK, d)),
                  _resident((d, d))],
        out_specs=[tok_spec] * 6,
        out_shape=[act] * 6,
        scratch_shapes=[pltpu.VMEM((tm, d), F32), pltpu.VMEM((tm, d), BF16)],
        compiler_params=params,
        name="in_proj",
    )(x2, norm_in[0][None], w_in[0].astype(BF16), norm_v[0][None], w_s[0], b_full,
      w_o_gmlp[0].astype(BF16))

    hps = ATTN_HEADS_PER_STEP
    n_q = seq // ATTN_TILE
    order = jnp.asarray(np.array(_tile_order(n_q), dtype=np.int32).T)
    head_spec = pl.BlockSpec((seq, hps * B_HEAD_DIM), lambda b, h, order_ref: (b, h))
    yb = pl.pallas_call(
        _attn_kernel,
        grid_spec=pltpu.PrefetchScalarGridSpec(
            num_scalar_prefetch=1,
            grid=(bsz, B_HEADS // hps),
            in_specs=[head_spec] * 4,
            out_specs=head_spec,
            scratch_shapes=[pltpu.VMEM((n_q * hps, ATTN_TILE, B_HEAD_DIM), F32),
                            pltpu.VMEM((n_q * hps, ATTN_TILE, 1), F32),
                            *[pltpu.VMEM((hps, ATTN_TILE, ATTN_TILE), F32)] * 2]),
        out_shape=act,
        compiler_params=pltpu.CompilerParams(dimension_semantics=("arbitrary", "arbitrary"),
                                             vmem_limit_bytes=VMEM_LIMIT_BYTES),
        name="stick_breaking_attn",
    )(order, q, k, v, zb)

    assert tokens % OUT_TOKEN_TILE == 0
    out_spec = pl.BlockSpec((OUT_TOKEN_TILE, d), lambda i: (i, 0))
    out = pl.pallas_call(
        _out_kernel,
        grid=(tokens // OUT_TOKEN_TILE,),
        in_specs=[out_spec, out_spec, out_spec, out_spec,
                  _resident((d, d)), _resident((d, d)), _resident((1, d))],
        out_specs=out_spec,
        out_shape=jax.ShapeDtypeStruct((tokens, d), F32),
        compiler_params=params,
        name="out_proj",
    )(x2, yb, gapa, gb, w_o_sb[0].astype(BF16), w_out[0].astype(BF16), norm_final[None])
    return out.reshape(bsz, seq, d)
```

```python
import math

import jax
import numpy as np
import jax.numpy as jnp
from jax import lax
from jax.experimental import pallas as pl
from jax.experimental.pallas import tpu as pltpu

D_MODEL = 1024
CHUNK = 128
A_GROUPS = 8
A_GROUP_DIM = D_MODEL // A_GROUPS
B_HEADS = 8
B_HEAD_DIM = D_MODEL // B_HEADS
EPS = 1e-6
N_SEGMENTS = 9
SEG_U, SEG_V, SEG_ZA, SEG_Q, SEG_K, SEG_VB, SEG_ZB, SEG_GA, SEG_GB = range(N_SEGMENTS)
LOG2_E = math.log2(math.e)
EXP2_CLAMP = 64.0

TOKEN_TILE = 512
IN_SUBTILE = 256
OUT_TOKEN_TILE = 1024
OUT_SUBTILE = 256
ATTN_TILE = 256
ATTN_HEADS_PER_STEP = 4
ATTN_LOOP_UNROLL = 4
VMEM_LIMIT_BYTES = 56 * 1024 * 1024

F32 = jnp.float32
BF16 = jnp.bfloat16


def _rms_norm(x, gain):
    return x * lax.rsqrt(jnp.mean(x * x, axis=-1, keepdims=True) + EPS) * gain


def _in_proj_kernel(x_ref, norm_in_ref, w_in_ref, norm_v_ref, w_s_ref, b_ref, w_oa_ref,
                    gapa_ref, q_ref, k_ref, v_ref, zb_ref, gb_ref, ug_ref, ya_ref):
    tm = x_ref.shape[0]
    parts = [slice(r, r + IN_SUBTILE) for r in range(0, tm, IN_SUBTILE)]
    hb = [_rms_norm(x_ref[p, :], norm_in_ref[...]).astype(BF16) for p in parts]

    def seg(i):
        w = w_in_ref[:, i * D_MODEL:(i + 1) * D_MODEL]
        return [jnp.dot(h, w, preferred_element_type=F32) for h in hb]

    def store(ref, values, fn):
        for p, val in zip(parts, values):
            ref[p, :] = fn(val).astype(BF16)

    v_a = seg(SEG_V)
    u_a = seg(SEG_U)
    z_a = seg(SEG_ZA)
    store(q_ref, seg(SEG_Q), lambda s: s * (B_HEAD_DIM ** -0.5 * LOG2_E))
    store(k_ref, seg(SEG_K), lambda s: s)
    vn = [_rms_norm(jax.nn.gelu(s), norm_v_ref[...]).astype(BF16) for s in v_a]
    for p, u, z in zip(parts, u_a, z_a):
        ug_ref[p, :] = jax.nn.gelu(u) * jax.nn.silu(z)

    row = lax.broadcasted_iota(jnp.int32, (CHUNK, CHUNK), 0)
    col = lax.broadcasted_iota(jnp.int32, (CHUNK, CHUNK), 1)
    causal = col <= row
    for g in range(A_GROUPS):
        w_g = jnp.where(causal, w_s_ref[g], 0.0).astype(BF16)
        cols = slice(g * A_GROUP_DIM, (g + 1) * A_GROUP_DIM)
        for p, vn_p in zip(parts, vn):
            for c in range(0, IN_SUBTILE, CHUNK):
                rows = slice(p.start + c, p.start + c + CHUNK)
                mixed = (jnp.dot(w_g, vn_p[c:c + CHUNK, cols], preferred_element_type=F32)
                         + b_ref[:, cols])
                ya_ref[rows, cols] = (ug_ref[rows, cols] * mixed).astype(BF16)

    store(zb_ref, seg(SEG_ZB), jax.nn.silu)
    store(gb_ref, seg(SEG_GB), jax.nn.sigmoid)
    p_a = [jnp.dot(ya_ref[p, :], w_oa_ref[...], preferred_element_type=F32) for p in parts]
    store(gapa_ref, [jax.nn.sigmoid(g) * pa for g, pa in zip(seg(SEG_GA), p_a)], lambda s: s)
    store(v_ref, seg(SEG_VB), lambda s: s)


def _tile_order(n_q):
    return [(i, i - d) for d in range(n_q) for i in range(d, n_q)]


def _attn_kernel(order_ref, q_ref, k_ref, v_ref, zb_ref, o_ref,
                 acc_ref, carry_ref, z0_ref, z1_ref, w0_ref, w1_ref):
    seq, width = q_ref.shape
    heads = width // B_HEAD_DIM
    hs = range(heads)
    cols = [slice(h * B_HEAD_DIM, (h + 1) * B_HEAD_DIM) for h in hs]
    t = ATTN_TILE
    n_q = seq // t
    order = _tile_order(n_q)
    n_tiles = len(order)
    z_ref = (z0_ref, z1_ref)
    w_ref = (w0_ref, w1_ref)
    row = lax.broadcasted_iota(jnp.int32, (t, t), 0)
    col = lax.broadcasted_iota(jnp.int32, (t, t), 1)
    strictly_before = col < row
    neg_from = jnp.where(row >= col, -1.0, 0.0).astype(BF16)

    def rows_of(idx):
        if isinstance(idx, int):
            return slice(idx * t, (idx + 1) * t)
        return pl.ds(pl.multiple_of(idx * t, t), t)

    def logits(slot, h, i, j):
        z_ref[slot][h] = lax.dot_general(
            q_ref[rows_of(i), cols[h]], k_ref[rows_of(j), cols[h]],
            (((1,), (1,)), ((), ())), preferred_element_type=F32)

    def neg_log_keep(slot, diagonal):
        out = []
        for h in hs:
            z = z_ref[slot][h]
            n = jnp.maximum(z, jnp.log(1.0 + jnp.exp2(jnp.minimum(z, EXP2_CLAMP))) * LOG2_E)
            if diagonal:
                n = jnp.where(strictly_before, n, 0.0)
            out.append(n.astype(BF16))
        return out

    def weights(slot, suffix, i, diagonal):
        for h in hs:
            x = z_ref[slot][h] + suffix[h]
            if not diagonal:
                x = x + carry_ref[i * heads + h]
            p = jnp.exp2(x)
            if diagonal:
                p = jnp.where(strictly_before, p, 0.0)
            w_ref[slot][h] = p.astype(BF16)
            tile_total = suffix[h][:, 0:1]
            if diagonal:
                carry_ref[i * heads + h] = tile_total
            else:
                carry_ref[i * heads + h] += tile_total

    def iteration(it, parity, tile_of, diagonal_of):
        old, cur, nxt = tile_of(it - 2), tile_of(it - 1), tile_of(it)
        if cur is not None:
            nl = neg_log_keep(1 - parity, diagonal_of(it - 1))
        suffix, pv = [], []
        for h in hs:
            if cur is not None:
                suffix.append(jnp.dot(nl[h], neg_from, preferred_element_type=F32))
            if nxt is not None:
                logits(parity, h, *nxt)
            if old is not None:
                pv.append(jnp.dot(w_ref[parity][h], v_ref[rows_of(old[1]), cols[h]],
                                  preferred_element_type=F32))
        if old is not None:
            for h in hs:
                if diagonal_of(it - 2):
                    acc_ref[old[0] * heads + h] = pv[h]
                else:
                    acc_ref[old[0] * heads + h] += pv[h]
        if cur is not None:
            weights(1 - parity, suffix, cur[0], diagonal_of(it - 1))

    def static_tile(n):
        return order[n] if 0 <= n < n_tiles else None

    def static_diag(n):
        return n < n_q

    def traced_tile(n):
        return order_ref[0, n], order_ref[1, n]

    unroll = ATTN_LOOP_UNROLL
    loop_start = n_q + 2
    n_trips = max(0, (n_tiles - loop_start) // unroll)
    loop_end = loop_start + unroll * n_trips
    for it in range(0, loop_start):
        iteration(it, it % 2, static_tile, static_diag)

    def trip(p, _):
        for u in range(unroll):
            iteration(loop_start + unroll * p + u, (loop_start + u) % 2, traced_tile,
                      lambda n: False)
        return 0

    lax.fori_loop(0, n_trips, trip, 0)
    for it in range(loop_end, n_tiles + 2):
        iteration(it, it % 2, static_tile, static_diag)

    for i in range(n_q):
        for h in hs:
            o_ref[rows_of(i), cols[h]] = (acc_ref[i * heads + h]
                                          * zb_ref[rows_of(i), cols[h]].astype(F32)).astype(BF16)


def _out_kernel(x_ref, yb_ref, gapa_ref, gb_ref, w_ob_ref, w_out_ref, norm_ref, o_ref):
    tm = x_ref.shape[0]
    rows = [slice(r, r + OUT_SUBTILE) for r in range(0, tm, OUT_SUBTILE)]
    p_b = [jnp.dot(yb_ref[r, :], w_ob_ref[...], preferred_element_type=F32) for r in rows]
    merged = [(gapa_ref[r, :].astype(F32) + gb_ref[r, :].astype(F32) * p).astype(BF16)
              for r, p in zip(rows, p_b)]
    update = [jnp.dot(m, w_out_ref[...], preferred_element_type=F32) for m in merged]
    for r, u in zip(rows, update):
        o_ref[r, :] = _rms_norm(x_ref[r, :] + u, norm_ref[...])


def _resident(shape):
    return pl.BlockSpec(shape, lambda *_: (0,) * len(shape), pipeline_mode=pl.Buffered(1))


def kernel(x, norm_in, w_in, norm_v, w_s, b_s, w_o_gmlp, w_o_sb, w_out, norm_final):
    bsz, seq, d = x.shape
    assert d == D_MODEL and seq % ATTN_TILE == 0 and seq % CHUNK == 0
    tokens = bsz * seq
    tm = TOKEN_TILE
    assert tokens % tm == 0 and tm % CHUNK == 0
    x2 = x.reshape(tokens, d)
    b_full = jnp.repeat(b_s[0].T, A_GROUP_DIM, axis=1)

    tok_spec = pl.BlockSpec((tm, d), lambda i: (i, 0))
    act = jax.ShapeDtypeStruct((tokens, d), BF16)
    params = pltpu.CompilerParams(dimension_semantics=("arbitrary",),
                                  vmem_limit_bytes=VMEM_LIMIT_BYTES)

    gapa, q, k, v, zb, gb = pl.pallas_call(
        _in_proj_kernel,
        grid=(tokens // tm,),
        in_specs=[tok_spec,
                  _resident((1, d)),
                  _resident((d, N_SEGMENTS * d)),
                  _resident((1, d)),
                  _resident((A_GROUPS, CHUNK, CHUNK)),
                  _resident((CHUNK, d)),
                  _resident((d, d))],
        out_specs=[tok_spec] * 6,
        out_shape=[act] * 6,
        scratch_shapes=[pltpu.VMEM((tm, d), F32), pltpu.VMEM((tm, d), BF16)],
        compiler_params=params,
        name="in_proj",
    )(x2, norm_in[0][None], w_in[0].astype(BF16), norm_v[0][None], w_s[0], b_full,
      w_o_gmlp[0].astype(BF16))

    hps = ATTN_HEADS_PER_STEP
    n_q = seq // ATTN_TILE
    order = jnp.asarray(np.array(_tile_order(n_q), dtype=np.int32).T)
    head_spec = pl.BlockSpec((seq, hps * B_HEAD_DIM), lambda b, h, order_ref: (b, h))
    yb = pl.pallas_call(
        _attn_kernel,
        grid_spec=pltpu.PrefetchScalarGridSpec(
            num_scalar_prefetch=1,
            grid=(bsz, B_HEADS // hps),
            in_specs=[head_spec] * 4,
            out_specs=head_spec,
            scratch_shapes=[pltpu.VMEM((n_q * hps, ATTN_TILE, B_HEAD_DIM), F32),
                            pltpu.VMEM((n_q * hps, ATTN_TILE, 1), F32),
                            *[pltpu.VMEM((hps, ATTN_TILE, ATTN_TILE), F32)] * 2,
                            *[pltpu.VMEM((hps, ATTN_TILE, ATTN_TILE), BF16)] * 2]),
        out_shape=act,
        compiler_params=pltpu.CompilerParams(dimension_semantics=("arbitrary", "arbitrary"),
                                             vmem_limit_bytes=VMEM_LIMIT_BYTES),
        name="stick_breaking_attn",
    )(order, q, k, v, zb)

    assert tokens % OUT_TOKEN_TILE == 0
    out_spec = pl.BlockSpec((OUT_TOKEN_TILE, d), lambda i: (i, 0))
    out = pl.pallas_call(
        _out_kernel,
        grid=(tokens // OUT_TOKEN_TILE,),
        in_specs=[out_spec, out_spec, out_spec, out_spec,
                  _resident((d, d)), _resident((d, d)), _resident((1, d))],
        out_specs=out_spec,
        out_shape=jax.ShapeDtypeStruct((tokens, d), F32),
        compiler_params=params,
        name="out_proj",
    )(x2, yb, gapa, gb, w_o_sb[0].astype(BF16), w_out[0].astype(BF16), norm_final[None])
    return out.reshape(bsz, seq, d)
```

```python
import math

import jax
import numpy as np
import jax.numpy as jnp
from jax import lax
from jax.experimental import pallas as pl
from jax.experimental.pallas import tpu as pltpu

D_MODEL = 1024
CHUNK = 128
A_GROUPS = 8
A_GROUP_DIM = D_MODEL // A_GROUPS
B_HEADS = 8
B_HEAD_DIM = D_MODEL // B_HEADS
EPS = 1e-6
N_SEGMENTS = 9
SEG_U, SEG_V, SEG_ZA, SEG_Q, SEG_K, SEG_VB, SEG_ZB, SEG_GA, SEG_GB = range(N_SEGMENTS)
LOG2_E = math.log2(math.e)
EXP2_CLAMP = 64.0

TOKEN_TILE = 512
IN_SUBTILE = 256
OUT_TOKEN_TILE = 1024
OUT_SUBTILE = 256
ATTN_TILE = 256
ATTN_HEADS_PER_STEP = 4
ATTN_LOOP_UNROLL = 4
VMEM_LIMIT_BYTES = 56 * 1024 * 1024

F32 = jnp.float32
BF16 = jnp.bfloat16


def _rms_norm(x, gain):
    return x * lax.rsqrt(jnp.mean(x * x, axis=-1, keepdims=True) + EPS) * gain


def _in_proj_kernel(x_ref, norm_in_ref, w_in_ref, norm_v_ref, w_s_ref, b_ref, w_oa_ref,
                    gapa_ref, q_ref, k_ref, v_ref, zb_ref, gb_ref, ug_ref, ya_ref):
    tm = x_ref.shape[0]
    parts = [slice(r, r + IN_SUBTILE) for r in range(0, tm, IN_SUBTILE)]
    hb = [_rms_norm(x_ref[p, :], norm_in_ref[...]).astype(BF16) for p in parts]

    def seg(i):
        w = w_in_ref[:, i * D_MODEL:(i + 1) * D_MODEL]
        return [jnp.dot(h, w, preferred_element_type=F32) for h in hb]

    def store(ref, values, fn):
        for p, val in zip(parts, values):
            ref[p, :] = fn(val).astype(BF16)

    v_a = seg(SEG_V)
    u_a = seg(SEG_U)
    z_a = seg(SEG_ZA)
    store(q_ref, seg(SEG_Q), lambda s: s * (B_HEAD_DIM ** -0.5 * LOG2_E))
    store(k_ref, seg(SEG_K), lambda s: s)
    vn = [_rms_norm(jax.nn.gelu(s), norm_v_ref[...]).astype(BF16) for s in v_a]
    for p, u, z in zip(parts, u_a, z_a):
        ug_ref[p, :] = jax.nn.gelu(u) * jax.nn.silu(z)

    row = lax.broadcasted_iota(jnp.int32, (CHUNK, CHUNK), 0)
    col = lax.broadcasted_iota(jnp.int32, (CHUNK, CHUNK), 1)
    causal = col <= row
    for g in range(A_GROUPS):
        w_g = jnp.where(causal, w_s_ref[g], 0.0).astype(BF16)
        cols = slice(g * A_GROUP_DIM, (g + 1) * A_GROUP_DIM)
        for p, vn_p in zip(parts, vn):
            for c in range(0, IN_SUBTILE, CHUNK):
                rows = slice(p.start + c, p.start + c + CHUNK)
                mixed = (jnp.dot(w_g, vn_p[c:c + CHUNK, cols], preferred_element_type=F32)
                         + b_ref[:, cols])
                ya_ref[rows, cols] = (ug_ref[rows, cols] * mixed).astype(BF16)

    store(zb_ref, seg(SEG_ZB), jax.nn.silu)
    store(gb_ref, seg(SEG_GB), jax.nn.sigmoid)
    p_a = [jnp.dot(ya_ref[p, :], w_oa_ref[...], preferred_element_type=F32) for p in parts]
    store(gapa_ref, [jax.nn.sigmoid(g) * pa for g, pa in zip(seg(SEG_GA), p_a)], lambda s: s)
    store(v_ref, seg(SEG_VB), lambda s: s)


def _tile_order(n_q):
    return [(i, i - d) for d in range(n_q) for i in range(d, n_q)]


def _attn_kernel(order_ref, q_ref, k_ref, v_ref, zb_ref, o_ref,
                 acc_ref, carry_ref, z0_ref, z1_ref, w0_ref, w1_ref, vt_ref):
    seq, width = q_ref.shape
    heads = width // B_HEAD_DIM
    hs = range(heads)
    cols = [slice(h * B_HEAD_DIM, (h + 1) * B_HEAD_DIM) for h in hs]
    t = ATTN_TILE
    n_q = seq // t
    order = _tile_order(n_q)
    n_tiles = len(order)
    z_ref = (z0_ref, z1_ref)
    w_ref = (w0_ref, w1_ref)
    row = lax.broadcasted_iota(jnp.int32, (t, t), 0)
    col = lax.broadcasted_iota(jnp.int32, (t, t), 1)
    strictly_before = row < col
    neg_from = jnp.where(col >= row, -1.0, 0.0).astype(BF16)

    def rows_of(idx):
        if isinstance(idx, int):
            return slice(idx * t, (idx + 1) * t)
        return pl.ds(pl.multiple_of(idx * t, t), t)

    for h in hs:
        for jt in range(n_q):
            vt_ref[h * n_q + jt] = v_ref[rows_of(jt), cols[h]].T

    def logits(slot, h, i, j):
        z_ref[slot][h] = lax.dot_general(
            k_ref[rows_of(j), cols[h]], q_ref[rows_of(i), cols[h]],
            (((1,), (1,)), ((), ())), preferred_element_type=F32)

    def neg_log_keep(slot, diagonal):
        out = []
        for h in hs:
            z = z_ref[slot][h]
            n = jnp.maximum(z, jnp.log(1.0 + jnp.exp2(jnp.minimum(z, EXP2_CLAMP))) * LOG2_E)
            if diagonal:
                n = jnp.where(strictly_before, n, 0.0)
            out.append(n.astype(BF16))
        return out

    def weights(slot, suffix, i, diagonal):
        for h in hs:
            x = z_ref[slot][h] + suffix[h]
            if not diagonal:
                x = x + carry_ref[i * heads + h]
            p = jnp.exp2(x)
            if diagonal:
                p = jnp.where(strictly_before, p, 0.0)
            w_ref[slot][h] = p.astype(BF16)
            tile_total = suffix[h][0:1, :]
            if diagonal:
                carry_ref[i * heads + h] = tile_total
            else:
                carry_ref[i * heads + h] += tile_total

    def iteration(it, parity, tile_of, diagonal_of):
        old, cur, nxt = tile_of(it - 2), tile_of(it - 1), tile_of(it)
        if cur is not None:
            nl = neg_log_keep(1 - parity, diagonal_of(it - 1))
        suffix, pv = [], []
        for h in hs:
            if cur is not None:
                suffix.append(jnp.dot(neg_from, nl[h], preferred_element_type=F32))
            if nxt is not None:
                logits(parity, h, *nxt)
            if old is not None:
                pv.append(jnp.dot(vt_ref[h * n_q + old[1]], w_ref[parity][h],
                                  preferred_element_type=F32))
        if old is not None:
            for h in hs:
                if diagonal_of(it - 2):
                    acc_ref[old[0] * heads + h] = pv[h]
                else:
                    acc_ref[old[0] * heads + h] += pv[h]
        if cur is not None:
            weights(1 - parity, suffix, cur[0], diagonal_of(it - 1))

    def static_tile(n):
        return order[n] if 0 <= n < n_tiles else None

    def static_diag(n):
        return n < n_q

    def traced_tile(n):
        return order_ref[0, n], order_ref[1, n]

    unroll = ATTN_LOOP_UNROLL
    loop_start = n_q + 2
    n_trips = max(0, (n_tiles - loop_start) // unroll)
    loop_end = loop_start + unroll * n_trips
    for it in range(0, loop_start):
        iteration(it, it % 2, static_tile, static_diag)

    def trip(p, _):
        for u in range(unroll):
            iteration(loop_start + unroll * p + u, (loop_start + u) % 2, traced_tile,
                      lambda n: False)
        return 0

    lax.fori_loop(0, n_trips, trip, 0)
    for it in range(loop_end, n_tiles + 2):
        iteration(it, it % 2, static_tile, static_diag)

    for i in range(n_q):
        for h in hs:
            o_ref[rows_of(i), cols[h]] = (acc_ref[i * heads + h].T
                                          * zb_ref[rows_of(i), cols[h]].astype(F32)).astype(BF16)


def _out_kernel(x_ref, yb_ref, gapa_ref, gb_ref, w_ob_ref, w_out_ref, norm_ref, o_ref):
    tm = x_ref.shape[0]
    rows = [slice(r, r + OUT_SUBTILE) for r in range(0, tm, OUT_SUBTILE)]
    p_b = [jnp.dot(yb_ref[r, :], w_ob_ref[...], preferred_element_type=F32) for r in rows]
    merged = [(gapa_ref[r, :].astype(F32) + gb_ref[r, :].astype(F32) * p).astype(BF16)
              for r, p in zip(rows, p_b)]
    update = [jnp.dot(m, w_out_ref[...], preferred_element_type=F32) for m in merged]
    for r, u in zip(rows, update):
        o_ref[r, :] = _rms_norm(x_ref[r, :] + u, norm_ref[...])


def _resident(shape):
    return pl.BlockSpec(shape, lambda *_: (0,) * len(shape), pipeline_mode=pl.Buffered(1))


def kernel(x, norm_in, w_in, norm_v, w_s, b_s, w_o_gmlp, w_o_sb, w_out, norm_final):
    bsz, seq, d = x.shape
    assert d == D_MODEL and seq % ATTN_TILE == 0 and seq % CHUNK == 0
    tokens = bsz * seq
    tm = TOKEN_TILE
    assert tokens % tm == 0 and tm % CHUNK == 0
    x2 = x.reshape(tokens, d)
    b_full = jnp.repeat(b_s[0].T, A_GROUP_DIM, axis=1)

    tok_spec = pl.BlockSpec((tm, d), lambda i: (i, 0))
    act = jax.ShapeDtypeStruct((tokens, d), BF16)
    params = pltpu.CompilerParams(dimension_semantics=("arbitrary",),
                                  vmem_limit_bytes=VMEM_LIMIT_BYTES)

    gapa, q, k, v, zb, gb = pl.pallas_call(
        _in_proj_kernel,
        grid=(tokens // tm,),
        in_specs=[tok_spec,
                  _resident((1, d)),
                  _resident((d, N_SEGMENTS * d)),
                  _resident((1, d)),
                  _resident((A_GROUPS, CHUNK, CHUNK)),
                  _resident((CHUNK, d)),
                  _resident((d, d))],
        out_specs=[tok_spec] * 6,
        out_shape=[act] * 6,
        scratch_shapes=[pltpu.VMEM((tm, d), F32), pltpu.VMEM((tm, d), BF16)],
        compiler_params=params,
        name="in_proj",
    )(x2, norm_in[0][None], w_in[0].astype(BF16), norm_v[0][None], w_s[0], b_full,
      w_o_gmlp[0].astype(BF16))

    hps = ATTN_HEADS_PER_STEP
    n_q = seq // ATTN_TILE
    order = jnp.asarray(np.array(_tile_order(n_q), dtype=np.int32).T)
    head_spec = pl.BlockSpec((seq, hps * B_HEAD_DIM), lambda b, h, order_ref: (b, h))
    yb = pl.pallas_call(
        _attn_kernel,
        grid_spec=pltpu.PrefetchScalarGridSpec(
            num_scalar_prefetch=1,
            grid=(bsz, B_HEADS // hps),
            in_specs=[head_spec] * 4,
            out_specs=head_spec,
            scratch_shapes=[pltpu.VMEM((n_q * hps, B_HEAD_DIM, ATTN_TILE), F32),
                            pltpu.VMEM((n_q * hps, 1, ATTN_TILE), F32),
                            *[pltpu.VMEM((hps, ATTN_TILE, ATTN_TILE), F32)] * 2,
                            *[pltpu.VMEM((hps, ATTN_TILE, ATTN_TILE), BF16)] * 2,
                            pltpu.VMEM((hps * n_q, B_HEAD_DIM, ATTN_TILE), BF16)]),
        out_shape=act,
        compiler_params=pltpu.CompilerParams(dimension_semantics=("arbitrary", "arbitrary"),
                                             vmem_limit_bytes=VMEM_LIMIT_BYTES),
        name="stick_breaking_attn",
    )(order, q, k, v, zb)

    assert tokens % OUT_TOKEN_TILE == 0
    out_spec = pl.BlockSpec((OUT_TOKEN_TILE, d), lambda i: (i, 0))
    out = pl.pallas_call(
        _out_kernel,
        grid=(tokens // OUT_TOKEN_TILE,),
        in_specs=[out_spec, out_spec, out_spec, out_spec,
                  _resident((d, d)), _resident((d, d)), _resident((1, d))],
        out_specs=out_spec,
        out_shape=jax.ShapeDtypeStruct((tokens, d), F32),
        compiler_params=params,
        name="out_proj",
    )(x2, yb, gapa, gb, w_o_sb[0].astype(BF16), w_out[0].astype(BF16), norm_final[None])
    return out.reshape(bsz, seq, d)
```

```python
import math

import jax
import numpy as np
import jax.numpy as jnp
from jax import lax
from jax.experimental import pallas as pl
from jax.experimental.pallas import tpu as pltpu

D_MODEL = 1024
CHUNK = 128
A_GROUPS = 8
A_GROUP_DIM = D_MODEL // A_GROUPS
B_HEADS = 8
B_HEAD_DIM = D_MODEL // B_HEADS
EPS = 1e-6
N_SEGMENTS = 9
SEG_U, SEG_V, SEG_ZA, SEG_Q, SEG_K, SEG_VB, SEG_ZB, SEG_GA, SEG_GB = range(N_SEGMENTS)
LOG2_E = math.log2(math.e)
EXP2_CLAMP = 64.0

TOKEN_TILE = 512
IN_SUBTILE = 256
OUT_TOKEN_TILE = 1024
OUT_SUBTILE = 256
ATTN_TILE = 256
ATTN_HEADS_PER_STEP = 4
ATTN_LOOP_UNROLL = 6
VMEM_LIMIT_BYTES = 56 * 1024 * 1024

F32 = jnp.float32
BF16 = jnp.bfloat16


def _rms_norm(x, gain):
    return x * lax.rsqrt(jnp.mean(x * x, axis=-1, keepdims=True) + EPS) * gain


def _in_proj_kernel(x_ref, norm_in_ref, w_in_ref, norm_v_ref, w_s_ref, b_ref, w_oa_ref,
                    gapa_ref, q_ref, k_ref, v_ref, zb_ref, gb_ref, ug_ref, ya_ref):
    tm = x_ref.shape[0]
    parts = [slice(r, r + IN_SUBTILE) for r in range(0, tm, IN_SUBTILE)]
    hb = [_rms_norm(x_ref[p, :], norm_in_ref[...]).astype(BF16) for p in parts]

    def seg(i):
        w = w_in_ref[:, i * D_MODEL:(i + 1) * D_MODEL]
        return [jnp.dot(h, w, preferred_element_type=F32) for h in hb]

    def store(ref, values, fn):
        for p, val in zip(parts, values):
            ref[p, :] = fn(val).astype(BF16)

    v_a = seg(SEG_V)
    u_a = seg(SEG_U)
    z_a = seg(SEG_ZA)
    store(q_ref, seg(SEG_Q), lambda s: s * (B_HEAD_DIM ** -0.5 * LOG2_E))
    store(k_ref, seg(SEG_K), lambda s: s)
    vn = [_rms_norm(jax.nn.gelu(s), norm_v_ref[...]).astype(BF16) for s in v_a]
    for p, u, z in zip(parts, u_a, z_a):
        ug_ref[p, :] = jax.nn.gelu(u) * jax.nn.silu(z)

    row = lax.broadcasted_iota(jnp.int32, (CHUNK, CHUNK), 0)
    col = lax.broadcasted_iota(jnp.int32, (CHUNK, CHUNK), 1)
    causal = col <= row
    for g in range(A_GROUPS):
        w_g = jnp.where(causal, w_s_ref[g], 0.0).astype(BF16)
        cols = slice(g * A_GROUP_DIM, (g + 1) * A_GROUP_DIM)
        for p, vn_p in zip(parts, vn):
            for c in range(0, IN_SUBTILE, CHUNK):
                rows = slice(p.start + c, p.start + c + CHUNK)
                mixed = (jnp.dot(w_g, vn_p[c:c + CHUNK, cols], preferred_element_type=F32)
                         + b_ref[:, cols])
                ya_ref[rows, cols] = (ug_ref[rows, cols] * mixed).astype(BF16)

    store(zb_ref, seg(SEG_ZB), jax.nn.silu)
    store(gb_ref, seg(SEG_GB), jax.nn.sigmoid)
    p_a = [jnp.dot(ya_ref[p, :], w_oa_ref[...], preferred_element_type=F32) for p in parts]
    store(gapa_ref, [jax.nn.sigmoid(g) * pa for g, pa in zip(seg(SEG_GA), p_a)], lambda s: s)
    store(v_ref, seg(SEG_VB), lambda s: s)


def _tile_order(n_q):
    return [(i, i - d) for d in range(n_q) for i in range(d, n_q)]


def _attn_kernel(order_ref, q_ref, k_ref, v_ref, zb_ref, o_ref,
                 acc_ref, carry_ref, z0_ref, z1_ref, w0_ref, w1_ref, vt_ref):
    seq, width = q_ref.shape
    heads = width // B_HEAD_DIM
    hs = range(heads)
    cols = [slice(h * B_HEAD_DIM, (h + 1) * B_HEAD_DIM) for h in hs]
    t = ATTN_TILE
    n_q = seq // t
    order = _tile_order(n_q)
    n_tiles = len(order)
    z_ref = (z0_ref, z1_ref)
    w_ref = (w0_ref, w1_ref)
    row = lax.broadcasted_iota(jnp.int32, (t, t), 0)
    col = lax.broadcasted_iota(jnp.int32, (t, t), 1)
    strictly_before = row < col
    neg_from = jnp.where(col >= row, -1.0, 0.0).astype(BF16)

    def rows_of(idx):
        if isinstance(idx, int):
            return slice(idx * t, (idx + 1) * t)
        return pl.ds(pl.multiple_of(idx * t, t), t)

    for h in hs:
        for jt in range(n_q):
            vt_ref[h * n_q + jt] = v_ref[rows_of(jt), cols[h]].T

    def logits(slot, h, i, j):
        z_ref[slot][h] = lax.dot_general(
            k_ref[rows_of(j), cols[h]], q_ref[rows_of(i), cols[h]],
            (((1,), (1,)), ((), ())), preferred_element_type=F32)

    def neg_log_keep(slot, diagonal):
        out = []
        for h in hs:
            z = z_ref[slot][h]
            n = jnp.maximum(z, jnp.log(1.0 + jnp.exp2(jnp.minimum(z, EXP2_CLAMP))) * LOG2_E)
            if diagonal:
                n = jnp.where(strictly_before, n, 0.0)
            out.append(n.astype(BF16))
        return out

    def weights(slot, suffix, i, diagonal):
        for h in hs:
            x = z_ref[slot][h] + suffix[h]
            if not diagonal:
                x = x + carry_ref[i * heads + h]
            p = jnp.exp2(x)
            if diagonal:
                p = jnp.where(strictly_before, p, 0.0)
            w_ref[slot][h] = p.astype(BF16)
            tile_total = suffix[h][0:1, :]
            if diagonal:
                carry_ref[i * heads + h] = tile_total
            else:
                carry_ref[i * heads + h] += tile_total

    def iteration(it, parity, tile_of, diagonal_of):
        old, cur, nxt = tile_of(it - 2), tile_of(it - 1), tile_of(it)
        if cur is not None:
            nl = neg_log_keep(1 - parity, diagonal_of(it - 1))
        suffix, pv = [], []
        for h in hs:
            if cur is not None:
                suffix.append(jnp.dot(neg_from, nl[h], preferred_element_type=F32))
            if nxt is not None:
                logits(parity, h, *nxt)
            if old is not None:
                pv.append(jnp.dot(vt_ref[h * n_q + old[1]], w_ref[parity][h],
                                  preferred_element_type=F32))
        if old is not None:
            for h in hs:
                if diagonal_of(it - 2):
                    acc_ref[old[0] * heads + h] = pv[h]
                else:
                    acc_ref[old[0] * heads + h] += pv[h]
        if cur is not None:
            weights(1 - parity, suffix, cur[0], diagonal_of(it - 1))

    def static_tile(n):
        return order[n] if 0 <= n < n_tiles else None

    def static_diag(n):
        return n < n_q

    def traced_tile(n):
        return order_ref[0, n], order_ref[1, n]

    unroll = ATTN_LOOP_UNROLL
    loop_start = n_q + 2
    n_trips = max(0, (n_tiles - loop_start) // unroll)
    loop_end = loop_start + unroll * n_trips
    for it in range(0, loop_start):
        iteration(it, it % 2, static_tile, static_diag)

    def trip(p, _):
        for u in range(unroll):
            iteration(loop_start + unroll * p + u, (loop_start + u) % 2, traced_tile,
                      lambda n: False)
        return 0

    lax.fori_loop(0, n_trips, trip, 0)
    for it in range(loop_end, n_tiles + 2):
        iteration(it, it % 2, static_tile, static_diag)

    for i in range(n_q):
        for h in hs:
            o_ref[rows_of(i), cols[h]] = (acc_ref[i * heads + h].T
                                          * zb_ref[rows_of(i), cols[h]].astype(F32)).astype(BF16)


def _out_kernel(x_ref, yb_ref, gapa_ref, gb_ref, w_ob_ref, w_out_ref, norm_ref, o_ref):
    tm = x_ref.shape[0]
    rows = [slice(r, r + OUT_SUBTILE) for r in range(0, tm, OUT_SUBTILE)]
    p_b = [jnp.dot(yb_ref[r, :], w_ob_ref[...], preferred_element_type=F32) for r in rows]
    merged = [(gapa_ref[r, :].astype(F32) + gb_ref[r, :].astype(F32) * p).astype(BF16)
              for r, p in zip(rows, p_b)]
    update = [jnp.dot(m, w_out_ref[...], preferred_element_type=F32) for m in merged]
    for r, u in zip(rows, update):
        o_ref[r, :] = _rms_norm(x_ref[r, :] + u, norm_ref[...])


def _resident(shape):
    return pl.BlockSpec(shape, lambda *_: (0,) * len(shape), pipeline_mode=pl.Buffered(1))


def kernel(x, norm_in, w_in, norm_v, w_s, b_s, w_o_gmlp, w_o_sb, w_out, norm_final):
    bsz, seq, d = x.shape
    assert d == D_MODEL and seq % ATTN_TILE == 0 and seq % CHUNK == 0
    tokens = bsz * seq
    tm = TOKEN_TILE
    assert tokens % tm == 0 and tm % CHUNK == 0
    x2 = x.reshape(tokens, d)
    b_full = jnp.repeat(b_s[0].T, A_GROUP_DIM, axis=1)

    tok_spec = pl.BlockSpec((tm, d), lambda i: (i, 0))
    act = jax.ShapeDtypeStruct((tokens, d), BF16)
    params = pltpu.CompilerParams(dimension_semantics=("arbitrary",),
                                  vmem_limit_bytes=VMEM_LIMIT_BYTES)

    gapa, q, k, v, zb, gb = pl.pallas_call(
        _in_proj_kernel,
        grid=(tokens // tm,),
        in_specs=[tok_spec,
                  _resident((1, d)),
                  _resident((d, N_SEGMENTS * d)),
                  _resident((1, d)),
                  _resident((A_GROUPS, CHUNK, CHUNK)),
                  _resident((CHUNK, d)),
                  _resident((d, d))],
        out_specs=[tok_spec] * 6,
        out_shape=[act] * 6,
        scratch_shapes=[pltpu.VMEM((tm, d), F32), pltpu.VMEM((tm, d), BF16)],
        compiler_params=params,
        name="in_proj",
    )(x2, norm_in[0][None], w_in[0].astype(BF16), norm_v[0][None], w_s[0], b_full,
      w_o_gmlp[0].astype(BF16))

    hps = ATTN_HEADS_PER_STEP
    n_q = seq // ATTN_TILE
    order = jnp.asarray(np.array(_tile_order(n_q), dtype=np.int32).T)
    head_spec = pl.BlockSpec((seq, hps * B_HEAD_DIM), lambda b, h, order_ref: (b, h))
    yb = pl.pallas_call(
        _attn_kernel,
        grid_spec=pltpu.PrefetchScalarGridSpec(
            num_scalar_prefetch=1,
            grid=(bsz, B_HEADS // hps),
            in_specs=[head_spec] * 4,
            out_specs=head_spec,
            scratch_shapes=[pltpu.VMEM((n_q * hps, B_HEAD_DIM, ATTN_TILE), F32),
                            pltpu.VMEM((n_q * hps, 1, ATTN_TILE), F32),
                            *[pltpu.VMEM((hps, ATTN_TILE, ATTN_TILE), F32)] * 2,
                            *[pltpu.VMEM((hps, ATTN_TILE, ATTN_TILE), BF16)] * 2,
                            pltpu.VMEM((hps * n_q, B_HEAD_DIM, ATTN_TILE), BF16)]),
        out_shape=act,
        compiler_params=pltpu.CompilerParams(dimension_semantics=("arbitrary", "arbitrary"),
                                             vmem_limit_bytes=VMEM_LIMIT_BYTES),
        name="stick_breaking_attn",
    )(order, q, k, v, zb)

    assert tokens % OUT_TOKEN_TILE == 0
    out_spec = pl.BlockSpec((OUT_TOKEN_TILE, d), lambda i: (i, 0))
    out = pl.pallas_call(
        _out_kernel,
        grid=(tokens // OUT_TOKEN_TILE,),
        in_specs=[out_spec, out_spec, out_spec, out_spec,
                  _resident((d, d)), _resident((d, d)), _resident((1, d))],
        out_specs=out_spec,
        out_shape=jax.ShapeDtypeStruct((tokens, d), F32),
        compiler_params=params,
        name="out_proj",
    )(x2, yb, gapa, gb, w_o_sb[0].astype(BF16), w_out[0].astype(BF16), norm_final[None])
    return out.reshape(bsz, seq, d)
```

```python
import math

import jax
import numpy as np
import jax.numpy as jnp
from jax import lax
from jax.experimental import pallas as pl
from jax.experimental.pallas import tpu as pltpu

D_MODEL = 1024
CHUNK = 128
A_GROUPS = 8
A_GROUP_DIM = D_MODEL // A_GROUPS
B_HEADS = 8
B_HEAD_DIM = D_MODEL // B_HEADS
EPS = 1e-6
N_SEGMENTS = 9
SEG_U, SEG_V, SEG_ZA, SEG_Q, SEG_K, SEG_VB, SEG_ZB, SEG_GA, SEG_GB = range(N_SEGMENTS)
LOG2_E = math.log2(math.e)
EXP2_CLAMP = 64.0

TOKEN_TILE = 512
IN_SUBTILE = 256
OUT_TOKEN_TILE = 1024
OUT_SUBTILE = 256
ATTN_TILE = 256
ATTN_HEADS_PER_STEP = 4
ATTN_LOOP_UNROLL = 6
VMEM_LIMIT_BYTES = 56 * 1024 * 1024

F32 = jnp.float32
BF16 = jnp.bfloat16


def _rms_norm(x, gain):
    return x * lax.rsqrt(jnp.mean(x * x, axis=-1, keepdims=True) + EPS) * gain


def _in_proj_kernel(x_ref, norm_in_ref, w_in_ref, norm_v_ref, w_s_ref, b_ref, w_oa_ref,
                    gapa_ref, q_ref, k_ref, v_ref, zb_ref, gb_ref, ug_ref, ya_ref):
    tm = x_ref.shape[0]
    parts = [slice(r, r + IN_SUBTILE) for r in range(0, tm, IN_SUBTILE)]
    hb = [_rms_norm(x_ref[p, :], norm_in_ref[...]).astype(BF16) for p in parts]

    def seg(i):
        w = w_in_ref[:, i * D_MODEL:(i + 1) * D_MODEL]
        return [jnp.dot(h, w, preferred_element_type=F32) for h in hb]

    def store(ref, values, fn):
        for p, val in zip(parts, values):
            ref[p, :] = fn(val).astype(BF16)

    v_a = seg(SEG_V)
    u_a = seg(SEG_U)
    z_a = seg(SEG_ZA)
    store(q_ref, seg(SEG_Q), lambda s: s * (B_HEAD_DIM ** -0.5 * LOG2_E))
    store(k_ref, seg(SEG_K), lambda s: s)
    vn = [_rms_norm(jax.nn.gelu(s), norm_v_ref[...]).astype(BF16) for s in v_a]
    for p, u, z in zip(parts, u_a, z_a):
        ug_ref[p, :] = jax.nn.gelu(u) * jax.nn.silu(z)

    row = lax.broadcasted_iota(jnp.int32, (CHUNK, CHUNK), 0)
    col = lax.broadcasted_iota(jnp.int32, (CHUNK, CHUNK), 1)
    causal = col <= row
    for g in range(A_GROUPS):
        w_g = jnp.where(causal, w_s_ref[g], 0.0).astype(BF16)
        cols = slice(g * A_GROUP_DIM, (g + 1) * A_GROUP_DIM)
        for p, vn_p in zip(parts, vn):
            for c in range(0, IN_SUBTILE, CHUNK):
                rows = slice(p.start + c, p.start + c + CHUNK)
                mixed = (jnp.dot(w_g, vn_p[c:c + CHUNK, cols], preferred_element_type=F32)
                         + b_ref[:, cols])
                ya_ref[rows, cols] = (ug_ref[rows, cols] * mixed).astype(BF16)

    store(zb_ref, seg(SEG_ZB), jax.nn.silu)
    store(gb_ref, seg(SEG_GB), jax.nn.sigmoid)
    p_a = [jnp.dot(ya_ref[p, :], w_oa_ref[...], preferred_element_type=F32) for p in parts]
    store(gapa_ref, [jax.nn.sigmoid(g) * pa for g, pa in zip(seg(SEG_GA), p_a)], lambda s: s)
    store(v_ref, seg(SEG_VB), lambda s: s)


def _tile_order(n_q):
    return [(i, i - d) for d in range(n_q) for i in range(d, n_q)]


def _attn_kernel(order_ref, q_ref, k_ref, v_ref, zb_ref, o_ref,
                 acc_ref, carry_ref, z0_ref, z1_ref, w0_ref, w1_ref, vt_ref):
    seq, width = q_ref.shape
    heads = width // B_HEAD_DIM
    hs = range(heads)
    cols = [slice(h * B_HEAD_DIM, (h + 1) * B_HEAD_DIM) for h in hs]
    t = ATTN_TILE
    n_q = seq // t
    order = _tile_order(n_q)
    n_tiles = len(order)
    z_ref = (z0_ref, z1_ref)
    w_ref = (w0_ref, w1_ref)
    row = lax.broadcasted_iota(jnp.int32, (t, t), 0)
    col = lax.broadcasted_iota(jnp.int32, (t, t), 1)
    strictly_before = row < col
    neg_from = jnp.where(col >= row, -1.0, 0.0).astype(BF16)

    def rows_of(idx):
        if isinstance(idx, int):
            return slice(idx * t, (idx + 1) * t)
        return pl.ds(pl.multiple_of(idx * t, t), t)

    for h in hs:
        for jt in range(n_q):
            vt_ref[h * n_q + jt] = v_ref[rows_of(jt), cols[h]].T

    def logits(slot, h, i, j):
        z_ref[slot][h] = lax.dot_general(
            k_ref[rows_of(j), cols[h]], q_ref[rows_of(i), cols[h]],
            (((1,), (1,)), ((), ())), preferred_element_type=F32)

    def neg_log_keep(slot, diagonal):
        out = []
        for h in hs:
            z = z_ref[slot][h]
            n = jnp.maximum(z, jnp.log(1.0 + jnp.exp2(jnp.minimum(z, EXP2_CLAMP))) * LOG2_E)
            if diagonal:
                n = jnp.where(strictly_before, n, 0.0)
            out.append(n.astype(BF16))
        return out

    def weights(slot, suffix, i, diagonal):
        for h in hs:
            x = z_ref[slot][h] + suffix[h]
            if not diagonal:
                x = x + carry_ref[i * heads + h]
            p = jnp.exp2(x)
            if diagonal:
                p = jnp.where(strictly_before, p, 0.0)
            w_ref[slot][h] = jnp.minimum(p.astype(BF16), 1.0)
            tile_total = suffix[h][0:1, :]
            if diagonal:
                carry_ref[i * heads + h] = tile_total
            else:
                carry_ref[i * heads + h] += tile_total

    def iteration(it, parity, tile_of, diagonal_of):
        old, cur, nxt = tile_of(it - 2), tile_of(it - 1), tile_of(it)
        if cur is not None:
            nl = neg_log_keep(1 - parity, diagonal_of(it - 1))
        suffix, pv = [], []
        for h in hs:
            if cur is not None:
                suffix.append(jnp.dot(neg_from, nl[h], preferred_element_type=F32))
            if nxt is not None:
                logits(parity, h, *nxt)
            if old is not None:
                pv.append(jnp.dot(vt_ref[h * n_q + old[1]], w_ref[parity][h],
                                  preferred_element_type=F32))
        if old is not None:
            for h in hs:
                if diagonal_of(it - 2):
                    acc_ref[old[0] * heads + h] = pv[h]
                else:
                    acc_ref[old[0] * heads + h] += pv[h]
        if cur is not None:
            weights(1 - parity, suffix, cur[0], diagonal_of(it - 1))

    def static_tile(n):
        return order[n] if 0 <= n < n_tiles else None

    def static_diag(n):
        return n < n_q

    def traced_tile(n):
        return order_ref[0, n], order_ref[1, n]

    unroll = ATTN_LOOP_UNROLL
    loop_start = n_q + 2
    n_trips = max(0, (n_tiles - loop_start) // unroll)
    loop_end = loop_start + unroll * n_trips
    for it in range(0, loop_start):
        iteration(it, it % 2, static_tile, static_diag)

    def trip(p, _):
        for u in range(unroll):
            iteration(loop_start + unroll * p + u, (loop_start + u) % 2, traced_tile,
                      lambda n: False)
        return 0

    lax.fori_loop(0, n_trips, trip, 0)
    for it in range(loop_end, n_tiles + 2):
        iteration(it, it % 2, static_tile, static_diag)

    for i in range(n_q):
        for h in hs:
            o_ref[rows_of(i), cols[h]] = (acc_ref[i * heads + h].T
                                          * zb_ref[rows_of(i), cols[h]].astype(F32)).astype(BF16)


def _out_kernel(x_ref, yb_ref, gapa_ref, gb_ref, w_ob_ref, w_out_ref, norm_ref, o_ref):
    tm = x_ref.shape[0]
    rows = [slice(r, r + OUT_SUBTILE) for r in range(0, tm, OUT_SUBTILE)]
    p_b = [jnp.dot(yb_ref[r, :], w_ob_ref[...], preferred_element_type=F32) for r in rows]
    merged = [(gapa_ref[r, :].astype(F32) + gb_ref[r, :].astype(F32) * p).astype(BF16)
              for r, p in zip(rows, p_b)]
    update = [jnp.dot(m, w_out_ref[...], preferred_element_type=F32) for m in merged]
    for r, u in zip(rows, update):
        o_ref[r, :] = _rms_norm(x_ref[r, :] + u, norm_ref[...])


def _resident(shape):
    return pl.BlockSpec(shape, lambda *_: (0,) * len(shape), pipeline_mode=pl.Buffered(1))


def kernel(x, norm_in, w_in, norm_v, w_s, b_s, w_o_gmlp, w_o_sb, w_out, norm_final):
    bsz, seq, d = x.shape
    assert d == D_MODEL and seq % ATTN_TILE == 0 and seq % CHUNK == 0
    tokens = bsz * seq
    tm = TOKEN_TILE
    assert tokens % tm == 0 and tm % CHUNK == 0
    x2 = x.reshape(tokens, d)
    b_full = jnp.repeat(b_s[0].T, A_GROUP_DIM, axis=1)

    tok_spec = pl.BlockSpec((tm, d), lambda i: (i, 0))
    act = jax.ShapeDtypeStruct((tokens, d), BF16)
    params = pltpu.CompilerParams(dimension_semantics=("arbitrary",),
                                  vmem_limit_bytes=VMEM_LIMIT_BYTES)

    gapa, q, k, v, zb, gb = pl.pallas_call(
        _in_proj_kernel,
        grid=(tokens // tm,),
        in_specs=[tok_spec,
                  _resident((1, d)),
                  _resident((d, N_SEGMENTS * d)),
                  _resident((1, d)),
                  _resident((A_GROUPS, CHUNK, CHUNK)),
                  _resident((CHUNK, d)),
                  _resident((d, d))],
        out_specs=[tok_spec] * 6,
        out_shape=[act] * 6,
        scratch_shapes=[pltpu.VMEM((tm, d), F32), pltpu.VMEM((tm, d), BF16)],
        compiler_params=params,
        name="in_proj",
    )(x2, norm_in[0][None], w_in[0].astype(BF16), norm_v[0][None], w_s[0], b_full,
      w_o_gmlp[0].astype(BF16))

    hps = ATTN_HEADS_PER_STEP
    n_q = seq // ATTN_TILE
    order = jnp.asarray(np.array(_tile_order(n_q), dtype=np.int32).T)
    head_spec = pl.BlockSpec((seq, hps * B_HEAD_DIM), lambda b, h, order_ref: (b, h))
    yb = pl.pallas_call(
        _attn_kernel,
        grid_spec=pltpu.PrefetchScalarGridSpec(
            num_scalar_prefetch=1,
            grid=(bsz, B_HEADS // hps),
            in_specs=[head_spec] * 4,
            out_specs=head_spec,
            scratch_shapes=[pltpu.VMEM((n_q * hps, B_HEAD_DIM, ATTN_TILE), F32),
                            pltpu.VMEM((n_q * hps, 1, ATTN_TILE), F32),
                            *[pltpu.VMEM((hps, ATTN_TILE, ATTN_TILE), F32)] * 2,
                            *[pltpu.VMEM((hps, ATTN_TILE, ATTN_TILE), BF16)] * 2,
                            pltpu.VMEM((hps * n_q, B_HEAD_DIM, ATTN_TILE), BF16)]),
        out_shape=act,
        compiler_params=pltpu.CompilerParams(dimension_semantics=("arbitrary", "arbitrary"),
                                             vmem_limit_bytes=VMEM_LIMIT_BYTES),
        name="stick_breaking_attn",
    )(order, q, k, v, zb)

    assert tokens % OUT_TOKEN_TILE == 0
    out_spec = pl.BlockSpec((OUT_TOKEN_TILE, d), lambda i: (i, 0))
    out = pl.pallas_call(
        _out_kernel,
        grid=(tokens // OUT_TOKEN_TILE,),
        in_specs=[out_spec, out_spec, out_spec, out_spec,
                  _resident((d, d)), _resident((d, d)), _resident((1, d))],
        out_specs=out_spec,
        out_shape=jax.ShapeDtypeStruct((tokens, d), F32),
        compiler_params=params,
        name="out_proj",
    )(x2, yb, gapa, gb, w_o_sb[0].astype(BF16), w_out[0].astype(BF16), norm_final[None])
    return out.reshape(bsz, seq, d)
```

```python
import math

import jax
import numpy as np
import jax.numpy as jnp
from jax import lax
from jax.experimental import pallas as pl
from jax.experimental.pallas import tpu as pltpu

D_MODEL = 1024
CHUNK = 128
A_GROUPS = 8
A_GROUP_DIM = D_MODEL // A_GROUPS
B_HEADS = 8
B_HEAD_DIM = D_MODEL // B_HEADS
EPS = 1e-6
N_SEGMENTS = 9
SEG_U, SEG_V, SEG_ZA, SEG_Q, SEG_K, SEG_VB, SEG_ZB, SEG_GA, SEG_GB = range(N_SEGMENTS)
LOG2_E = math.log2(math.e)
EXP2_CLAMP = 64.0

TOKEN_TILE = 512
IN_SUBTILE = 256
OUT_TOKEN_TILE = 1024
OUT_RING = 3
OUT_SUBTILE = 256
ATTN_TILE = 256
ATTN_HEADS_PER_STEP = 4
ATTN_LOOP_UNROLL = 6
VMEM_LIMIT_BYTES = 56 * 1024 * 1024

F32 = jnp.float32
BF16 = jnp.bfloat16


def _rms_norm(x, gain):
    return x * lax.rsqrt(jnp.mean(x * x, axis=-1, keepdims=True) + EPS) * gain


def _in_proj_kernel(x_ref, norm_in_ref, w_in_ref, norm_v_ref, w_s_ref, b_ref, w_oa_ref,
                    gapa_ref, q_ref, k_ref, v_ref, zb_ref, gb_ref, ug_ref, ya_ref):
    tm = x_ref.shape[0]
    parts = [slice(r, r + IN_SUBTILE) for r in range(0, tm, IN_SUBTILE)]
    hb = [_rms_norm(x_ref[p, :], norm_in_ref[...]).astype(BF16) for p in parts]

    def seg(i):
        w = w_in_ref[:, i * D_MODEL:(i + 1) * D_MODEL]
        return [jnp.dot(h, w, preferred_element_type=F32) for h in hb]

    def store(ref, values, fn):
        for p, val in zip(parts, values):
            ref[p, :] = fn(val).astype(BF16)

    v_a = seg(SEG_V)
    u_a = seg(SEG_U)
    z_a = seg(SEG_ZA)
    store(q_ref, seg(SEG_Q), lambda s: s * (B_HEAD_DIM ** -0.5 * LOG2_E))
    store(k_ref, seg(SEG_K), lambda s: s)
    vn = [_rms_norm(jax.nn.gelu(s), norm_v_ref[...]).astype(BF16) for s in v_a]
    for p, u, z in zip(parts, u_a, z_a):
        ug_ref[p, :] = jax.nn.gelu(u) * jax.nn.silu(z)

    row = lax.broadcasted_iota(jnp.int32, (CHUNK, CHUNK), 0)
    col = lax.broadcasted_iota(jnp.int32, (CHUNK, CHUNK), 1)
    causal = col <= row
    for g in range(A_GROUPS):
        w_g = jnp.where(causal, w_s_ref[g], 0.0).astype(BF16)
        cols = slice(g * A_GROUP_DIM, (g + 1) * A_GROUP_DIM)
        for p, vn_p in zip(parts, vn):
            for c in range(0, IN_SUBTILE, CHUNK):
                rows = slice(p.start + c, p.start + c + CHUNK)
                mixed = (jnp.dot(w_g, vn_p[c:c + CHUNK, cols], preferred_element_type=F32)
                         + b_ref[:, cols])
                ya_ref[rows, cols] = (ug_ref[rows, cols] * mixed).astype(BF16)

    store(zb_ref, seg(SEG_ZB), jax.nn.silu)
    store(gb_ref, seg(SEG_GB), jax.nn.sigmoid)
    p_a = [jnp.dot(ya_ref[p, :], w_oa_ref[...], preferred_element_type=F32) for p in parts]
    store(gapa_ref, [jax.nn.sigmoid(g) * pa for g, pa in zip(seg(SEG_GA), p_a)], lambda s: s)
    store(v_ref, seg(SEG_VB), lambda s: s)


def _tile_order(n_q):
    return [(i, i - d) for d in range(n_q) for i in range(d, n_q)]


def _attn_kernel(order_ref, q_ref, k_ref, v_ref, zb_ref, o_ref,
                 acc_ref, carry_ref, z0_ref, z1_ref, w0_ref, w1_ref, vt_ref):
    seq, width = q_ref.shape
    heads = width // B_HEAD_DIM
    hs = range(heads)
    cols = [slice(h * B_HEAD_DIM, (h + 1) * B_HEAD_DIM) for h in hs]
    t = ATTN_TILE
    n_q = seq // t
    order = _tile_order(n_q)
    n_tiles = len(order)
    z_ref = (z0_ref, z1_ref)
    w_ref = (w0_ref, w1_ref)
    row = lax.broadcasted_iota(jnp.int32, (t, t), 0)
    col = lax.broadcasted_iota(jnp.int32, (t, t), 1)
    strictly_before = row < col
    neg_from = jnp.where(col >= row, -1.0, 0.0).astype(BF16)

    def rows_of(idx):
        if isinstance(idx, int):
            return slice(idx * t, (idx + 1) * t)
        return pl.ds(pl.multiple_of(idx * t, t), t)

    for h in hs:
        for jt in range(n_q):
            vt_ref[h * n_q + jt] = v_ref[rows_of(jt), cols[h]].T

    def logits(slot, h, i, j):
        z_ref[slot][h] = lax.dot_general(
            k_ref[rows_of(j), cols[h]], q_ref[rows_of(i), cols[h]],
            (((1,), (1,)), ((), ())), preferred_element_type=F32)

    def neg_log_keep(slot, diagonal):
        out = []
        for h in hs:
            z = z_ref[slot][h]
            n = jnp.maximum(z, jnp.log(1.0 + jnp.exp2(jnp.minimum(z, EXP2_CLAMP))) * LOG2_E)
            if diagonal:
                n = jnp.where(strictly_before, n, 0.0)
            out.append(n.astype(BF16))
        return out

    def weights(slot, suffix, i, diagonal):
        for h in hs:
            x = z_ref[slot][h] + suffix[h]
            if not diagonal:
                x = x + carry_ref[i * heads + h]
            p = jnp.exp2(x)
            if diagonal:
                p = jnp.where(strictly_before, p, 0.0)
            w_ref[slot][h] = jnp.minimum(p.astype(BF16), 1.0)
            tile_total = suffix[h][0:1, :]
            if diagonal:
                carry_ref[i * heads + h] = tile_total
            else:
                carry_ref[i * heads + h] += tile_total

    def iteration(it, parity, tile_of, diagonal_of):
        old, cur, nxt = tile_of(it - 2), tile_of(it - 1), tile_of(it)
        if cur is not None:
            nl = neg_log_keep(1 - parity, diagonal_of(it - 1))
        suffix, pv = [], []
        for h in hs:
            if cur is not None:
                suffix.append(jnp.dot(neg_from, nl[h], preferred_element_type=F32))
            if nxt is not None:
                logits(parity, h, *nxt)
            if old is not None:
                pv.append(jnp.dot(vt_ref[h * n_q + old[1]], w_ref[parity][h],
                                  preferred_element_type=F32))
        if old is not None:
            for h in hs:
                if diagonal_of(it - 2):
                    acc_ref[old[0] * heads + h] = pv[h]
                else:
                    acc_ref[old[0] * heads + h] += pv[h]
        if cur is not None:
            weights(1 - parity, suffix, cur[0], diagonal_of(it - 1))

    def static_tile(n):
        return order[n] if 0 <= n < n_tiles else None

    def static_diag(n):
        return n < n_q

    def traced_tile(n):
        return order_ref[0, n], order_ref[1, n]

    unroll = ATTN_LOOP_UNROLL
    loop_start = n_q + 2
    n_trips = max(0, (n_tiles - loop_start) // unroll)
    loop_end = loop_start + unroll * n_trips
    for it in range(0, loop_start):
        iteration(it, it % 2, static_tile, static_diag)

    def trip(p, _):
        for u in range(unroll):
            iteration(loop_start + unroll * p + u, (loop_start + u) % 2, traced_tile,
                      lambda n: False)
        return 0

    lax.fori_loop(0, n_trips, trip, 0)
    for it in range(loop_end, n_tiles + 2):
        iteration(it, it % 2, static_tile, static_diag)

    for i in range(n_q):
        for h in hs:
            o_ref[rows_of(i), cols[h]] = (acc_ref[i * heads + h].T
                                          * zb_ref[rows_of(i), cols[h]].astype(F32)).astype(BF16)


def _out_kernel(x_hbm, yb_hbm, gapa_hbm, gb_hbm, w_ob_ref, w_out_ref, norm_ref, o_ref,
                x_buf, yb_buf, gapa_buf, gb_buf, sems):
    step = pl.program_id(0)
    n_steps = pl.num_programs(0)
    tm = o_ref.shape[0]
    sources = (x_hbm, yb_hbm, gapa_hbm, gb_hbm)
    rings = (x_buf, yb_buf, gapa_buf, gb_buf)

    def tile_copy(s, k):
        slot = lax.rem(s, OUT_RING)
        rows = pl.ds(pl.multiple_of(s * tm, tm), tm)
        return pltpu.make_async_copy(sources[k].at[rows, :], rings[k].at[slot], sems.at[k, slot])

    @pl.when(step == 0)
    def _():
        for s in range(OUT_RING - 1):
            for k in range(len(sources)):
                tile_copy(s, k).start()

    @pl.when(step + (OUT_RING - 1) < n_steps)
    def _():
        for k in range(len(sources)):
            tile_copy(step + (OUT_RING - 1), k).start()

    for k in range(len(sources)):
        tile_copy(step, k).wait()

    slot = lax.rem(step, OUT_RING)
    rows = [slice(r, r + OUT_SUBTILE) for r in range(0, tm, OUT_SUBTILE)]
    p_b = [jnp.dot(yb_buf[slot, r, :], w_ob_ref[...], preferred_element_type=F32) for r in rows]
    merged = [(gapa_buf[slot, r, :].astype(F32) + gb_buf[slot, r, :].astype(F32) * p).astype(BF16)
              for r, p in zip(rows, p_b)]
    update = [jnp.dot(m, w_out_ref[...], preferred_element_type=F32) for m in merged]
    for r, u in zip(rows, update):
        o_ref[r, :] = _rms_norm(x_buf[slot, r, :] + u, norm_ref[...])


def _resident(shape):
    return pl.BlockSpec(shape, lambda *_: (0,) * len(shape), pipeline_mode=pl.Buffered(1))


def kernel(x, norm_in, w_in, norm_v, w_s, b_s, w_o_gmlp, w_o_sb, w_out, norm_final):
    bsz, seq, d = x.shape
    assert d == D_MODEL and seq % ATTN_TILE == 0 and seq % CHUNK == 0
    tokens = bsz * seq
    tm = TOKEN_TILE
    assert tokens % tm == 0 and tm % CHUNK == 0
    x2 = x.reshape(tokens, d)
    b_full = jnp.repeat(b_s[0].T, A_GROUP_DIM, axis=1)

    tok_spec = pl.BlockSpec((tm, d), lambda i: (i, 0))
    act = jax.ShapeDtypeStruct((tokens, d), BF16)
    params = pltpu.CompilerParams(dimension_semantics=("arbitrary",),
                                  vmem_limit_bytes=VMEM_LIMIT_BYTES)

    gapa, q, k, v, zb, gb = pl.pallas_call(
        _in_proj_kernel,
        grid=(tokens // tm,),
        in_specs=[tok_spec,
                  _resident((1, d)),
                  _resident((d, N_SEGMENTS * d)),
                  _resident((1, d)),
                  _resident((A_GROUPS, CHUNK, CHUNK)),
                  _resident((CHUNK, d)),
                  _resident((d, d))],
        out_specs=[tok_spec] * 6,
        out_shape=[act] * 6,
        scratch_shapes=[pltpu.VMEM((tm, d), F32), pltpu.VMEM((tm, d), BF16)],
        compiler_params=params,
        name="in_proj",
    )(x2, norm_in[0][None], w_in[0].astype(BF16), norm_v[0][None], w_s[0], b_full,
      w_o_gmlp[0].astype(BF16))

    hps = ATTN_HEADS_PER_STEP
    n_q = seq // ATTN_TILE
    order = jnp.asarray(np.array(_tile_order(n_q), dtype=np.int32).T)
    head_spec = pl.BlockSpec((seq, hps * B_HEAD_DIM), lambda b, h, order_ref: (b, h))
    yb = pl.pallas_call(
        _attn_kernel,
        grid_spec=pltpu.PrefetchScalarGridSpec(
            num_scalar_prefetch=1,
            grid=(bsz, B_HEADS // hps),
            in_specs=[head_spec] * 4,
            out_specs=head_spec,
            scratch_shapes=[pltpu.VMEM((n_q * hps, B_HEAD_DIM, ATTN_TILE), F32),
                            pltpu.VMEM((n_q * hps, 1, ATTN_TILE), F32),
                            *[pltpu.VMEM((hps, ATTN_TILE, ATTN_TILE), F32)] * 2,
                            *[pltpu.VMEM((hps, ATTN_TILE, ATTN_TILE), BF16)] * 2,
                            pltpu.VMEM((hps * n_q, B_HEAD_DIM, ATTN_TILE), BF16)]),
        out_shape=act,
        compiler_params=pltpu.CompilerParams(dimension_semantics=("arbitrary", "arbitrary"),
                                             vmem_limit_bytes=VMEM_LIMIT_BYTES),
        name="stick_breaking_attn",
    )(order, q, k, v, zb)

    to = OUT_TOKEN_TILE
    assert tokens % to == 0 and tokens // to >= OUT_RING - 1
    in_hbm = pl.BlockSpec(memory_space=pl.ANY)
    out = pl.pallas_call(
        _out_kernel,
        grid=(tokens // to,),
        in_specs=[in_hbm, in_hbm, in_hbm, in_hbm,
                  _resident((d, d)), _resident((d, d)), _resident((1, d))],
        out_specs=pl.BlockSpec((to, d), lambda i: (i, 0)),
        out_shape=jax.ShapeDtypeStruct((tokens, d), F32),
        scratch_shapes=[pltpu.VMEM((OUT_RING, to, d), F32),
                        *[pltpu.VMEM((OUT_RING, to, d), BF16)] * 3,
                        pltpu.SemaphoreType.DMA((4, OUT_RING))],
        compiler_params=params,
        name="out_proj",
    )(x2, yb, gapa, gb, w_o_sb[0].astype(BF16), w_out[0].astype(BF16), norm_final[None])
    return out.reshape(bsz, seq, d)
```

```python
import math

import jax
import numpy as np
import jax.numpy as jnp
from jax import lax
from jax.experimental import pallas as pl
from jax.experimental.pallas import tpu as pltpu

D_MODEL = 1024
CHUNK = 128
A_GROUPS = 8
A_GROUP_DIM = D_MODEL // A_GROUPS
B_HEADS = 8
B_HEAD_DIM = D_MODEL // B_HEADS
EPS = 1e-6
N_SEGMENTS = 9
SEG_U, SEG_V, SEG_ZA, SEG_Q, SEG_K, SEG_VB, SEG_ZB, SEG_GA, SEG_GB = range(N_SEGMENTS)
LOG2_E = math.log2(math.e)
EXP2_CLAMP = 64.0

TOKEN_TILE = 512
IN_SUBTILE = 256
OUT_TOKEN_TILE = 1024
OUT_RING = 3
OUT_SUBTILE = 256
ATTN_TILE = 256
ATTN_HEADS_PER_STEP = 4
ATTN_LOOP_UNROLL = 6
VMEM_LIMIT_BYTES = 56 * 1024 * 1024

F32 = jnp.float32
BF16 = jnp.bfloat16


def _rms_norm(x, gain):
    return x * lax.rsqrt(jnp.mean(x * x, axis=-1, keepdims=True) + EPS) * gain


def _in_proj_kernel(x_ref, norm_in_ref, w_in_hbm, norm_v_ref, w_s_ref, b_ref, w_oa_hbm,
                    gapa_ref, q_ref, k_ref, v_ref, zb_ref, gb_ref,
                    ug_ref, ya_ref, w_in_ref, w_oa_ref, stage_ref, stage_sems):
    @pl.when(pl.program_id(0) == 0)
    def _():
        def segment_copy(c):
            src = (w_in_hbm.at[:, c * D_MODEL:(c + 1) * D_MODEL] if c < N_SEGMENTS else w_oa_hbm)
            return pltpu.make_async_copy(src, stage_ref.at[c % 2], stage_sems.at[c % 2])

        segment_copy(0).start()
        for c in range(N_SEGMENTS + 1):
            if c < N_SEGMENTS:
                segment_copy(c + 1).start()
            segment_copy(c).wait()
            if c < N_SEGMENTS:
                w_in_ref[:, c * D_MODEL:(c + 1) * D_MODEL] = stage_ref[c % 2].astype(BF16)
            else:
                w_oa_ref[...] = stage_ref[c % 2].astype(BF16)

    tm = x_ref.shape[0]
    parts = [slice(r, r + IN_SUBTILE) for r in range(0, tm, IN_SUBTILE)]
    hb = [_rms_norm(x_ref[p, :], norm_in_ref[...]).astype(BF16) for p in parts]

    def seg(i):
        w = w_in_ref[:, i * D_MODEL:(i + 1) * D_MODEL]
        return [jnp.dot(h, w, preferred_element_type=F32) for h in hb]

    def store(ref, values, fn):
        for p, val in zip(parts, values):
            ref[p, :] = fn(val).astype(BF16)

    v_a = seg(SEG_V)
    u_a = seg(SEG_U)
    z_a = seg(SEG_ZA)
    store(q_ref, seg(SEG_Q), lambda s: s * (B_HEAD_DIM ** -0.5 * LOG2_E))
    store(k_ref, seg(SEG_K), lambda s: s)
    vn = [_rms_norm(jax.nn.gelu(s), norm_v_ref[...]).astype(BF16) for s in v_a]
    for p, u, z in zip(parts, u_a, z_a):
        ug_ref[p, :] = jax.nn.gelu(u) * jax.nn.silu(z)

    row = lax.broadcasted_iota(jnp.int32, (CHUNK, CHUNK), 0)
    col = lax.broadcasted_iota(jnp.int32, (CHUNK, CHUNK), 1)
    causal = col <= row
    for g in range(A_GROUPS):
        w_g = jnp.where(causal, w_s_ref[g], 0.0).astype(BF16)
        cols = slice(g * A_GROUP_DIM, (g + 1) * A_GROUP_DIM)
        for p, vn_p in zip(parts, vn):
            for c in range(0, IN_SUBTILE, CHUNK):
                rows = slice(p.start + c, p.start + c + CHUNK)
                mixed = (jnp.dot(w_g, vn_p[c:c + CHUNK, cols], preferred_element_type=F32)
                         + b_ref[:, cols])
                ya_ref[rows, cols] = (ug_ref[rows, cols] * mixed).astype(BF16)

    store(zb_ref, seg(SEG_ZB), jax.nn.silu)
    store(gb_ref, seg(SEG_GB), jax.nn.sigmoid)
    p_a = [jnp.dot(ya_ref[p, :], w_oa_ref[...], preferred_element_type=F32) for p in parts]
    store(gapa_ref, [jax.nn.sigmoid(g) * pa for g, pa in zip(seg(SEG_GA), p_a)], lambda s: s)
    store(v_ref, seg(SEG_VB), lambda s: s)


def _tile_order(n_q):
    return [(i, i - d) for d in range(n_q) for i in range(d, n_q)]


def _attn_kernel(order_ref, q_ref, k_ref, v_ref, zb_ref, o_ref,
                 acc_ref, carry_ref, z0_ref, z1_ref, w0_ref, w1_ref, vt_ref):
    seq, width = q_ref.shape
    heads = width // B_HEAD_DIM
    hs = range(heads)
    cols = [slice(h * B_HEAD_DIM, (h + 1) * B_HEAD_DIM) for h in hs]
    t = ATTN_TILE
    n_q = seq // t
    order = _tile_order(n_q)
    n_tiles = len(order)
    z_ref = (z0_ref, z1_ref)
    w_ref = (w0_ref, w1_ref)
    row = lax.broadcasted_iota(jnp.int32, (t, t), 0)
    col = lax.broadcasted_iota(jnp.int32, (t, t), 1)
    strictly_before = row < col
    neg_from = jnp.where(col >= row, -1.0, 0.0).astype(BF16)

    def rows_of(idx):
        if isinstance(idx, int):
            return slice(idx * t, (idx + 1) * t)
        return pl.ds(pl.multiple_of(idx * t, t), t)

    for h in hs:
        for jt in range(n_q):
            vt_ref[h * n_q + jt] = v_ref[rows_of(jt), cols[h]].T

    def logits(slot, h, i, j):
        z_ref[slot][h] = lax.dot_general(
            k_ref[rows_of(j), cols[h]], q_ref[rows_of(i), cols[h]],
            (((1,), (1,)), ((), ())), preferred_element_type=F32)

    def neg_log_keep(slot, diagonal):
        out = []
        for h in hs:
            z = z_ref[slot][h]
            n = jnp.maximum(z, jnp.log(1.0 + jnp.exp2(jnp.minimum(z, EXP2_CLAMP))) * LOG2_E)
            if diagonal:
                n = jnp.where(strictly_before, n, 0.0)
            out.append(n.astype(BF16))
        return out

    def weights(slot, suffix, i, diagonal):
        for h in hs:
            x = z_ref[slot][h] + suffix[h]
            if not diagonal:
                x = x + carry_ref[i * heads + h]
            p = jnp.exp2(x)
            if diagonal:
                p = jnp.where(strictly_before, p, 0.0)
            w_ref[slot][h] = jnp.minimum(p.astype(BF16), 1.0)
            tile_total = suffix[h][0:1, :]
            if diagonal:
                carry_ref[i * heads + h] = tile_total
            else:
                carry_ref[i * heads + h] += tile_total

    def iteration(it, parity, tile_of, diagonal_of):
        old, cur, nxt = tile_of(it - 2), tile_of(it - 1), tile_of(it)
        if cur is not None:
            nl = neg_log_keep(1 - parity, diagonal_of(it - 1))
        suffix, pv = [], []
        for h in hs:
            if cur is not None:
                suffix.append(jnp.dot(neg_from, nl[h], preferred_element_type=F32))
            if nxt is not None:
                logits(parity, h, *nxt)
            if old is not None:
                pv.append(jnp.dot(vt_ref[h * n_q + old[1]], w_ref[parity][h],
                                  preferred_element_type=F32))
        if old is not None:
            for h in hs:
                if diagonal_of(it - 2):
                    acc_ref[old[0] * heads + h] = pv[h]
                else:
                    acc_ref[old[0] * heads + h] += pv[h]
        if cur is not None:
            weights(1 - parity, suffix, cur[0], diagonal_of(it - 1))

    def static_tile(n):
        return order[n] if 0 <= n < n_tiles else None

    def static_diag(n):
        return n < n_q

    def traced_tile(n):
        return order_ref[0, n], order_ref[1, n]

    unroll = ATTN_LOOP_UNROLL
    loop_start = n_q + 2
    n_trips = max(0, (n_tiles - loop_start) // unroll)
    loop_end = loop_start + unroll * n_trips
    for it in range(0, loop_start):
        iteration(it, it % 2, static_tile, static_diag)

    def trip(p, _):
        for u in range(unroll):
            iteration(loop_start + unroll * p + u, (loop_start + u) % 2, traced_tile,
                      lambda n: False)
        return 0

    lax.fori_loop(0, n_trips, trip, 0)
    for it in range(loop_end, n_tiles + 2):
        iteration(it, it % 2, static_tile, static_diag)

    for i in range(n_q):
        for h in hs:
            o_ref[rows_of(i), cols[h]] = (acc_ref[i * heads + h].T
                                          * zb_ref[rows_of(i), cols[h]].astype(F32)).astype(BF16)


def _out_kernel(x_hbm, yb_hbm, gapa_hbm, gb_hbm, w_ob_ref, w_out_ref, norm_ref, o_ref,
                x_buf, yb_buf, gapa_buf, gb_buf, sems):
    step = pl.program_id(0)
    n_steps = pl.num_programs(0)
    tm = o_ref.shape[0]
    sources = (x_hbm, yb_hbm, gapa_hbm, gb_hbm)
    rings = (x_buf, yb_buf, gapa_buf, gb_buf)

    def tile_copy(s, k):
        slot = lax.rem(s, OUT_RING)
        rows = pl.ds(pl.multiple_of(s * tm, tm), tm)
        return pltpu.make_async_copy(sources[k].at[rows, :], rings[k].at[slot], sems.at[k, slot])

    @pl.when(step == 0)
    def _():
        for s in range(OUT_RING - 1):
            for k in range(len(sources)):
                tile_copy(s, k).start()

    @pl.when(step + (OUT_RING - 1) < n_steps)
    def _():
        for k in range(len(sources)):
            tile_copy(step + (OUT_RING - 1), k).start()

    for k in range(len(sources)):
        tile_copy(step, k).wait()

    slot = lax.rem(step, OUT_RING)
    rows = [slice(r, r + OUT_SUBTILE) for r in range(0, tm, OUT_SUBTILE)]
    p_b = [jnp.dot(yb_buf[slot, r, :], w_ob_ref[...], preferred_element_type=F32) for r in rows]
    merged = [(gapa_buf[slot, r, :].astype(F32) + gb_buf[slot, r, :].astype(F32) * p).astype(BF16)
              for r, p in zip(rows, p_b)]
    update = [jnp.dot(m, w_out_ref[...], preferred_element_type=F32) for m in merged]
    for r, u in zip(rows, update):
        o_ref[r, :] = _rms_norm(x_buf[slot, r, :] + u, norm_ref[...])


def _resident(shape):
    return pl.BlockSpec(shape, lambda *_: (0,) * len(shape), pipeline_mode=pl.Buffered(1))


def kernel(x, norm_in, w_in, norm_v, w_s, b_s, w_o_gmlp, w_o_sb, w_out, norm_final):
    bsz, seq, d = x.shape
    assert d == D_MODEL and seq % ATTN_TILE == 0 and seq % CHUNK == 0
    tokens = bsz * seq
    tm = TOKEN_TILE
    assert tokens % tm == 0 and tm % CHUNK == 0
    x2 = x.reshape(tokens, d)
    b_full = jnp.repeat(b_s[0].T, A_GROUP_DIM, axis=1)

    tok_spec = pl.BlockSpec((tm, d), lambda i: (i, 0))
    act = jax.ShapeDtypeStruct((tokens, d), BF16)
    params = pltpu.CompilerParams(dimension_semantics=("arbitrary",),
                                  vmem_limit_bytes=VMEM_LIMIT_BYTES)

    gapa, q, k, v, zb, gb = pl.pallas_call(
        _in_proj_kernel,
        grid=(tokens // tm,),
        in_specs=[tok_spec,
                  _resident((1, d)),
                  pl.BlockSpec(memory_space=pl.ANY),
                  _resident((1, d)),
                  _resident((A_GROUPS, CHUNK, CHUNK)),
                  _resident((CHUNK, d)),
                  pl.BlockSpec(memory_space=pl.ANY)],
        out_specs=[tok_spec] * 6,
        out_shape=[act] * 6,
        scratch_shapes=[pltpu.VMEM((tm, d), F32), pltpu.VMEM((tm, d), BF16),
                        pltpu.VMEM((d, N_SEGMENTS * d), BF16), pltpu.VMEM((d, d), BF16),
                        pltpu.VMEM((2, d, d), F32), pltpu.SemaphoreType.DMA((2,))],
        compiler_params=params,
        name="in_proj",
    )(x2, norm_in[0][None], w_in[0], norm_v[0][None], w_s[0], b_full, w_o_gmlp[0])

    hps = ATTN_HEADS_PER_STEP
    n_q = seq // ATTN_TILE
    order = jnp.asarray(np.array(_tile_order(n_q), dtype=np.int32).T)
    head_spec = pl.BlockSpec((seq, hps * B_HEAD_DIM), lambda b, h, order_ref: (b, h))
    yb = pl.pallas_call(
        _attn_kernel,
        grid_spec=pltpu.PrefetchScalarGridSpec(
            num_scalar_prefetch=1,
            grid=(bsz, B_HEADS // hps),
            in_specs=[head_spec] * 4,
            out_specs=head_spec,
            scratch_shapes=[pltpu.VMEM((n_q * hps, B_HEAD_DIM, ATTN_TILE), F32),
                            pltpu.VMEM((n_q * hps, 1, ATTN_TILE), F32),
                            *[pltpu.VMEM((hps, ATTN_TILE, ATTN_TILE), F32)] * 2,
                            *[pltpu.VMEM((hps, ATTN_TILE, ATTN_TILE), BF16)] * 2,
                            pltpu.VMEM((hps * n_q, B_HEAD_DIM, ATTN_TILE), BF16)]),
        out_shape=act,
        compiler_params=pltpu.CompilerParams(dimension_semantics=("arbitrary", "arbitrary"),
                                             vmem_limit_bytes=VMEM_LIMIT_BYTES),
        name="stick_breaking_attn",
    )(order, q, k, v, zb)

    to = OUT_TOKEN_TILE
    assert tokens % to == 0 and tokens // to >= OUT_RING - 1
    in_hbm = pl.BlockSpec(memory_space=pl.ANY)
    out = pl.pallas_call(
        _out_kernel,
        grid=(tokens // to,),
        in_specs=[in_hbm, in_hbm, in_hbm, in_hbm,
                  _resident((d, d)), _resident((d, d)), _resident((1, d))],
        out_specs=pl.BlockSpec((to, d), lambda i: (i, 0)),
        out_shape=jax.ShapeDtypeStruct((tokens, d), F32),
        scratch_shapes=[pltpu.VMEM((OUT_RING, to, d), F32),
                        *[pltpu.VMEM((OUT_RING, to, d), BF16)] * 3,
                        pltpu.SemaphoreType.DMA((4, OUT_RING))],
        compiler_params=params,
        name="out_proj",
    )(x2, yb, gapa, gb, w_o_sb[0].astype(BF16), w_out[0].astype(BF16), norm_final[None])
    return out.reshape(bsz, seq, d)
```

```python
import math

import jax
import numpy as np
import jax.numpy as jnp
from jax import lax
from jax.experimental import pallas as pl
from jax.experimental.pallas import tpu as pltpu

D_MODEL = 1024
CHUNK = 128
A_GROUPS = 8
A_GROUP_DIM = D_MODEL // A_GROUPS
B_HEADS = 8
B_HEAD_DIM = D_MODEL // B_HEADS
EPS = 1e-6
N_SEGMENTS = 9
SEG_U, SEG_V, SEG_ZA, SEG_Q, SEG_K, SEG_VB, SEG_ZB, SEG_GA, SEG_GB = range(N_SEGMENTS)
LOG2_E = math.log2(math.e)
EXP2_CLAMP = 64.0

TOKEN_TILE = 512
IN_SUBTILE = 256
OUT_TOKEN_TILE = 1024
OUT_RING = 3
OUT_DMA_PRIORITY = (0, 1, 1, 1)
OUT_SUBTILE = 256
ATTN_TILE = 256
ATTN_HEADS_PER_STEP = 4
ATTN_LOOP_UNROLL = 6
VMEM_LIMIT_BYTES = 56 * 1024 * 1024

F32 = jnp.float32
BF16 = jnp.bfloat16


def _rms_norm(x, gain):
    return x * lax.rsqrt(jnp.mean(x * x, axis=-1, keepdims=True) + EPS) * gain


def _in_proj_kernel(x_ref, norm_in_ref, w_in_ref, norm_v_ref, w_s_ref, b_ref, w_oa_ref,
                    gapa_ref, q_ref, k_ref, v_ref, zb_ref, gb_ref, ug_ref, ya_ref):
    tm = x_ref.shape[0]
    parts = [slice(r, r + IN_SUBTILE) for r in range(0, tm, IN_SUBTILE)]
    hb = [_rms_norm(x_ref[p, :], norm_in_ref[...]).astype(BF16) for p in parts]

    def seg(i):
        w = w_in_ref[:, i * D_MODEL:(i + 1) * D_MODEL]
        return [jnp.dot(h, w, preferred_element_type=F32) for h in hb]

    def store(ref, values, fn):
        for p, val in zip(parts, values):
            ref[p, :] = fn(val).astype(BF16)

    v_a = seg(SEG_V)
    u_a = seg(SEG_U)
    z_a = seg(SEG_ZA)
    store(q_ref, seg(SEG_Q), lambda s: s * (B_HEAD_DIM ** -0.5 * LOG2_E))
    store(k_ref, seg(SEG_K), lambda s: s)
    vn = [_rms_norm(jax.nn.gelu(s), norm_v_ref[...]).astype(BF16) for s in v_a]
    for p, u, z in zip(parts, u_a, z_a):
        ug_ref[p, :] = jax.nn.gelu(u) * jax.nn.silu(z)

    row = lax.broadcasted_iota(jnp.int32, (CHUNK, CHUNK), 0)
    col = lax.broadcasted_iota(jnp.int32, (CHUNK, CHUNK), 1)
    causal = col <= row
    for g in range(A_GROUPS):
        w_g = jnp.where(causal, w_s_ref[g], 0.0).astype(BF16)
        cols = slice(g * A_GROUP_DIM, (g + 1) * A_GROUP_DIM)
        for p, vn_p in zip(parts, vn):
            for c in range(0, IN_SUBTILE, CHUNK):
                rows = slice(p.start + c, p.start + c + CHUNK)
                mixed = (jnp.dot(w_g, vn_p[c:c + CHUNK, cols], preferred_element_type=F32)
                         + b_ref[:, cols])
                ya_ref[rows, cols] = (ug_ref[rows, cols] * mixed).astype(BF16)

    store(zb_ref, seg(SEG_ZB), jax.nn.silu)
    store(gb_ref, seg(SEG_GB), jax.nn.sigmoid)
    p_a = [jnp.dot(ya_ref[p, :], w_oa_ref[...], preferred_element_type=F32) for p in parts]
    store(gapa_ref, [jax.nn.sigmoid(g) * pa for g, pa in zip(seg(SEG_GA), p_a)], lambda s: s)
    store(v_ref, seg(SEG_VB), lambda s: s)


def _tile_order(n_q):
    return [(i, i - d) for d in range(n_q) for i in range(d, n_q)]


def _attn_kernel(order_ref, q_ref, k_ref, v_ref, zb_ref, o_ref,
                 acc_ref, carry_ref, z0_ref, z1_ref, w0_ref, w1_ref, vt_ref):
    seq, width = q_ref.shape
    heads = width // B_HEAD_DIM
    hs = range(heads)
    cols = [slice(h * B_HEAD_DIM, (h + 1) * B_HEAD_DIM) for h in hs]
    t = ATTN_TILE
    n_q = seq // t
    order = _tile_order(n_q)
    n_tiles = len(order)
    z_ref = (z0_ref, z1_ref)
    w_ref = (w0_ref, w1_ref)
    row = lax.broadcasted_iota(jnp.int32, (t, t), 0)
    col = lax.broadcasted_iota(jnp.int32, (t, t), 1)
    strictly_before = row < col
    neg_from = jnp.where(col >= row, -1.0, 0.0).astype(BF16)

    def rows_of(idx):
        if isinstance(idx, int):
            return slice(idx * t, (idx + 1) * t)
        return pl.ds(pl.multiple_of(idx * t, t), t)

    for h in hs:
        for jt in range(n_q):
            vt_ref[h * n_q + jt] = v_ref[rows_of(jt), cols[h]].T

    def logits(slot, h, i, j):
        z_ref[slot][h] = lax.dot_general(
            k_ref[rows_of(j), cols[h]], q_ref[rows_of(i), cols[h]],
            (((1,), (1,)), ((), ())), preferred_element_type=F32)

    def neg_log_keep(slot, diagonal):
        out = []
        for h in hs:
            z = z_ref[slot][h]
            n = jnp.maximum(z, jnp.log(1.0 + jnp.exp2(jnp.minimum(z, EXP2_CLAMP))) * LOG2_E)
            if diagonal:
                n = jnp.where(strictly_before, n, 0.0)
            out.append(n.astype(BF16))
        return out

    def weights(slot, suffix, i, diagonal):
        for h in hs:
            x = z_ref[slot][h] + suffix[h]
            if not diagonal:
                x = x + carry_ref[i * heads + h]
            p = jnp.exp2(x)
            if diagonal:
                p = jnp.where(strictly_before, p, 0.0)
            w_ref[slot][h] = jnp.minimum(p.astype(BF16), 1.0)
            tile_total = suffix[h][0:1, :]
            if diagonal:
                carry_ref[i * heads + h] = tile_total
            else:
                carry_ref[i * heads + h] += tile_total

    def iteration(it, parity, tile_of, diagonal_of):
        old, cur, nxt = tile_of(it - 2), tile_of(it - 1), tile_of(it)
        if cur is not None:
            nl = neg_log_keep(1 - parity, diagonal_of(it - 1))
        suffix, pv = [], []
        for h in hs:
            if cur is not None:
                suffix.append(jnp.dot(neg_from, nl[h], preferred_element_type=F32))
            if nxt is not None:
                logits(parity, h, *nxt)
            if old is not None:
                pv.append(jnp.dot(vt_ref[h * n_q + old[1]], w_ref[parity][h],
                                  preferred_element_type=F32))
        if old is not None:
            for h in hs:
                if diagonal_of(it - 2):
                    acc_ref[old[0] * heads + h] = pv[h]
                else:
                    acc_ref[old[0] * heads + h] += pv[h]
        if cur is not None:
            weights(1 - parity, suffix, cur[0], diagonal_of(it - 1))

    def static_tile(n):
        return order[n] if 0 <= n < n_tiles else None

    def static_diag(n):
        return n < n_q

    def traced_tile(n):
        return order_ref[0, n], order_ref[1, n]

    unroll = ATTN_LOOP_UNROLL
    loop_start = n_q + 2
    n_trips = max(0, (n_tiles - loop_start) // unroll)
    loop_end = loop_start + unroll * n_trips
    for it in range(0, loop_start):
        iteration(it, it % 2, static_tile, static_diag)

    def trip(p, _):
        for u in range(unroll):
            iteration(loop_start + unroll * p + u, (loop_start + u) % 2, traced_tile,
                      lambda n: False)
        return 0

    lax.fori_loop(0, n_trips, trip, 0)
    for it in range(loop_end, n_tiles + 2):
        iteration(it, it % 2, static_tile, static_diag)

    for i in range(n_q):
        for h in hs:
            o_ref[rows_of(i), cols[h]] = (acc_ref[i * heads + h].T
                                          * zb_ref[rows_of(i), cols[h]].astype(F32)).astype(BF16)


def _out_kernel(x_hbm, yb_hbm, gapa_hbm, gb_hbm, w_ob_ref, w_out_ref, norm_ref, o_ref,
                x_buf, yb_buf, gapa_buf, gb_buf, sems):
    step = pl.program_id(0)
    n_steps = pl.num_programs(0)
    tm = o_ref.shape[0]
    sources = (x_hbm, yb_hbm, gapa_hbm, gb_hbm)
    rings = (x_buf, yb_buf, gapa_buf, gb_buf)

    def tile_copy(s, k):
        slot = lax.rem(s, OUT_RING)
        rows = pl.ds(pl.multiple_of(s * tm, tm), tm)
        return pltpu.make_async_copy(sources[k].at[rows, :], rings[k].at[slot], sems.at[k, slot])

    @pl.when(step == 0)
    def _():
        for s in range(OUT_RING - 1):
            for k in range(len(sources)):
                tile_copy(s, k).start(priority=OUT_DMA_PRIORITY[k])

    @pl.when(step + (OUT_RING - 1) < n_steps)
    def _():
        for k in range(len(sources)):
            tile_copy(step + (OUT_RING - 1), k).start(priority=OUT_DMA_PRIORITY[k])

    for k in range(len(sources)):
        tile_copy(step, k).wait()

    slot = lax.rem(step, OUT_RING)
    rows = [slice(r, r + OUT_SUBTILE) for r in range(0, tm, OUT_SUBTILE)]
    p_b = [jnp.dot(yb_buf[slot, r, :], w_ob_ref[...], preferred_element_type=F32) for r in rows]
    merged = [(gapa_buf[slot, r, :].astype(F32) + gb_buf[slot, r, :].astype(F32) * p).astype(BF16)
              for r, p in zip(rows, p_b)]
    update = [jnp.dot(m, w_out_ref[...], preferred_element_type=F32) for m in merged]
    for r, u in zip(rows, update):
        o_ref[r, :] = _rms_norm(x_buf[slot, r, :] + u, norm_ref[...])


def _resident(shape):
    return pl.BlockSpec(shape, lambda *_: (0,) * len(shape), pipeline_mode=pl.Buffered(1))


def kernel(x, norm_in, w_in, norm_v, w_s, b_s, w_o_gmlp, w_o_sb, w_out, norm_final):
    bsz, seq, d = x.shape
    assert d == D_MODEL and seq % ATTN_TILE == 0 and seq % CHUNK == 0
    tokens = bsz * seq
    tm = TOKEN_TILE
    assert tokens % tm == 0 and tm % CHUNK == 0
    x2 = x.reshape(tokens, d)
    b_full = jnp.repeat(b_s[0].T, A_GROUP_DIM, axis=1)

    tok_spec = pl.BlockSpec((tm, d), lambda i: (i, 0))
    act = jax.ShapeDtypeStruct((tokens, d), BF16)
    params = pltpu.CompilerParams(dimension_semantics=("arbitrary",),
                                  vmem_limit_bytes=VMEM_LIMIT_BYTES)

    gapa, q, k, v, zb, gb = pl.pallas_call(
        _in_proj_kernel,
        grid=(tokens // tm,),
        in_specs=[tok_spec,
                  _resident((1, d)),
                  _resident((d, N_SEGMENTS * d)),
                  _resident((1, d)),
                  _resident((A_GROUPS, CHUNK, CHUNK)),
                  _resident((CHUNK, d)),
                  _resident((d, d))],
        out_specs=[tok_spec] * 6,
        out_shape=[act] * 6,
        scratch_shapes=[pltpu.VMEM((tm, d), F32), pltpu.VMEM((tm, d), BF16)],
        compiler_params=params,
        name="in_proj",
    )(x2, norm_in[0][None], w_in[0].astype(BF16), norm_v[0][None], w_s[0], b_full,
      w_o_gmlp[0].astype(BF16))

    hps = ATTN_HEADS_PER_STEP
    n_q = seq // ATTN_TILE
    order = jnp.asarray(np.array(_tile_order(n_q), dtype=np.int32).T)
    head_spec = pl.BlockSpec((seq, hps * B_HEAD_DIM), lambda b, h, order_ref: (b, h))
    yb = pl.pallas_call(
        _attn_kernel,
        grid_spec=pltpu.PrefetchScalarGridSpec(
            num_scalar_prefetch=1,
            grid=(bsz, B_HEADS // hps),
            in_specs=[head_spec] * 4,
            out_specs=head_spec,
            scratch_shapes=[pltpu.VMEM((n_q * hps, B_HEAD_DIM, ATTN_TILE), F32),
                            pltpu.VMEM((n_q * hps, 1, ATTN_TILE), F32),
                            *[pltpu.VMEM((hps, ATTN_TILE, ATTN_TILE), F32)] * 2,
                            *[pltpu.VMEM((hps, ATTN_TILE, ATTN_TILE), BF16)] * 2,
                            pltpu.VMEM((hps * n_q, B_HEAD_DIM, ATTN_TILE), BF16)]),
        out_shape=act,
        compiler_params=pltpu.CompilerParams(dimension_semantics=("arbitrary", "arbitrary"),
                                             vmem_limit_bytes=VMEM_LIMIT_BYTES),
        name="stick_breaking_attn",
    )(order, q, k, v, zb)

    to = OUT_TOKEN_TILE
    assert tokens % to == 0 and tokens // to >= OUT_RING - 1
    in_hbm = pl.BlockSpec(memory_space=pl.ANY)
    out = pl.pallas_call(
        _out_kernel,
        grid=(tokens // to,),
        in_specs=[in_hbm, in_hbm, in_hbm, in_hbm,
                  _resident((d, d)), _resident((d, d)), _resident((1, d))],
        out_specs=pl.BlockSpec((to, d), lambda i: (i, 0)),
        out_shape=jax.ShapeDtypeStruct((tokens, d), F32),
        scratch_shapes=[pltpu.VMEM((OUT_RING, to, d), F32),
                        *[pltpu.VMEM((OUT_RING, to, d), BF16)] * 3,
                        pltpu.SemaphoreType.DMA((4, OUT_RING))],
        compiler_params=params,
        name="out_proj",
    )(x2, yb, gapa, gb, w_o_sb[0].astype(BF16), w_out[0].astype(BF16), norm_final[None])
    return out.reshape(bsz, seq, d)
```
